```python
import jax, jax.numpy as jnp
from jax import lax
import numpy as np

D_MODEL = 2048
BATCH = 2
SEQ = 4096
DEPTH = 2

GLA_HEADS = 4
GLA_DK = 128
GLA_DV = 256
GLA_GATE_RANK = 16
GLA_GATE_TAU = 16.0
GLA_CHUNK = 64
FOX_HEADS = 8
FOX_DH = 128
DSA_Q_HEADS = 16
DSA_KV_HEADS = 4
DSA_DH = 128
IDX_HEADS = 16
IDX_DH = 64
DSA_TOPK_MAX = 256
N_GROUPS = 4
EXPERTS_PER_GROUP = 8
EXPERT_FF = 512
TOP_GROUPS = 1
TOP_EXPERTS = 2
Q_BLOCK = 128
ROPE_THETA = 10000.0
LN_EPS = 1e-5
DN_ALPHA = (2 * DEPTH) ** 0.25
DN_BETA = (8 * DEPTH) ** -0.25
MIX_WIDTH = GLA_HEADS * GLA_DV + FOX_HEADS * FOX_DH
DSA_WIDTH = DSA_Q_HEADS * DSA_DH

kernel_name = "hybrid_gla_fox_dsa_hmoe"

F32 = jnp.float32


def _even_sizes():
    return [GLA_HEADS * GLA_DK, GLA_HEADS * GLA_DK, GLA_HEADS * GLA_DV, GLA_HEADS * GLA_DV,
            GLA_GATE_RANK, FOX_HEADS * FOX_DH, FOX_HEADS * FOX_DH, FOX_HEADS * FOX_DH, FOX_HEADS]


def _odd_sizes():
    return [DSA_Q_HEADS * DSA_DH, DSA_KV_HEADS * DSA_DH, DSA_KV_HEADS * DSA_DH,
            IDX_HEADS * IDX_DH, IDX_DH, IDX_HEADS]


def _split(t, sizes):
    return jnp.split(t, np.cumsum(sizes)[:-1].tolist(), axis=-1)


def layer_norm(x, g, b):
    xf = x.astype(F32)
    mu = jnp.mean(xf, axis=-1, keepdims=True)
    var = jnp.mean(jnp.square(xf - mu), axis=-1, keepdims=True)
    y = (xf - mu) * lax.rsqrt(var + LN_EPS) * g.astype(F32) + b.astype(F32)
    return y.astype(x.dtype)


def rms_norm(x, g):
    xf = x.astype(F32)
    return xf * lax.rsqrt(jnp.mean(jnp.square(xf), axis=-1, keepdims=True) + LN_EPS) * g.astype(F32)


def rope(x, pos):
    half = x.shape[-1] // 2
    inv = ROPE_THETA ** (-jnp.arange(half, dtype=F32) / half)
    ang = pos.astype(F32)[:, None] * inv[None, :]
    cos = jnp.cos(ang)[None, :, None, :]
    sin = jnp.sin(ang)[None, :, None, :]
    x1, x2 = x[..., :half], x[..., half:]
    return jnp.concatenate([x1 * cos - x2 * sin, x2 * cos + x1 * sin], axis=-1)


def gla_chunked(q, k, v, log_a):
    B, S, H, DK = q.shape
    DV = v.shape[-1]
    C = GLA_CHUNK
    N = S // C

    def chunks(t):
        return t.reshape(B, N, C, H, t.shape[-1]).transpose(0, 3, 1, 2, 4)

    q, k, v, log_a = chunks(q * DK ** -0.5), chunks(k), chunks(v), chunks(log_a)
    b = jnp.cumsum(log_a, axis=3)
    b_last = b[:, :, :, -1:, :]
    q_dec = q * jnp.exp(b)
    k_inv = k * jnp.exp(-b)
    k_end = k * jnp.exp(b_last - b)
    causal = jnp.tril(jnp.ones((C, C), dtype=bool))
    scores = jnp.where(causal, jnp.einsum('bhnid,bhnjd->bhnij', q_dec, k_inv), 0.0)
    o_intra = jnp.einsum('bhnij,bhnjv->bhniv', scores, v)
    chunk_state = jnp.einsum('bhncd,bhncv->bhndv', k_end, v)
    chunk_decay = jnp.exp(b_last[:, :, :, 0, :])

    def step(state, inp):
        ds, dec = inp
        return dec[..., None] * state + ds, state

    init = jnp.zeros((B, H, DK, DV), F32)
    _, prev = lax.scan(step, init, (jnp.moveaxis(chunk_state, 2, 0), jnp.moveaxis(chunk_decay, 2, 0)))
    prev = jnp.moveaxis(prev, 0, 2)
    o = o_intra + jnp.einsum('bhncd,bhndv->bhncv', q_dec, prev)
    return o.transpose(0, 2, 3, 1, 4).reshape(B, S, H, DV)


def fox_attention(q, k, v, log_f):
    B, S, H, D = q.shape
    c = jnp.cumsum(log_f, axis=1).transpose(0, 2, 1)
    key_pos = jnp.arange(S)
    scale = D ** -0.5

    def block(i):
        start = i * Q_BLOCK
        qb = lax.dynamic_slice_in_dim(q, start, Q_BLOCK, axis=1)
        cb = lax.dynamic_slice_in_dim(c, start, Q_BLOCK, axis=2)
        q_pos = start + jnp.arange(Q_BLOCK)
        logits = jnp.einsum('bqhd,bshd->bhqs', qb, k) * scale + cb[..., None] - c[:, :, None, :]
        logits = jnp.where(key_pos[None, :] <= q_pos[:, None], logits, -jnp.inf)
        p = jax.nn.softmax(logits, axis=-1)
        return jnp.einsum('bhqs,bshd->bqhd', p, v)

    out = lax.map(block, jnp.arange(S // Q_BLOCK))
    return jnp.moveaxis(out, 0, 1).reshape(B, S, H, D)


def dsa_attention(q, k, v, iq, ik, iw):
    B, S, HQ, D = q.shape
    HKV = k.shape[2]
    G = HQ // HKV
    top_k = min(DSA_TOPK_MAX, S // 4)
    key_pos = jnp.arange(S)
    gather = jax.vmap(lambda seq, idx: seq[idx])

    def block(i):
        start = i * Q_BLOCK
        q_pos = start + jnp.arange(Q_BLOCK)
        iqb = lax.dynamic_slice_in_dim(iq, start, Q_BLOCK, axis=1)
        iwb = lax.dynamic_slice_in_dim(iw, start, Q_BLOCK, axis=1)
        rel = jax.nn.relu(jnp.einsum('bqhd,bsd->bqhs', iqb, ik)) * IDX_DH ** -0.5
        score = jnp.einsum('bqh,bqhs->bqs', iwb, rel)
        score = jnp.where(key_pos[None, :] <= q_pos[:, None], score, -jnp.inf)
        _, idx = lax.top_k(score, top_k)
        valid = idx <= q_pos[None, :, None]
        kg = gather(k, idx)
        vg = gather(v, idx)
        qb = lax.dynamic_slice_in_dim(q, start, Q_BLOCK, axis=1).reshape(B, Q_BLOCK, HKV, G, D)
        logits = jnp.einsum('bqhgd,bqkhd->bhgqk', qb, kg) * D ** -0.5
        logits = jnp.where(valid[:, None, None], logits, -jnp.inf)
        p = jax.nn.softmax(logits, axis=-1)
        o = jnp.einsum('bhgqk,bqkhd->bqhgd', p, vg)
        return o.reshape(B, Q_BLOCK, HQ, D)

    out = lax.map(block, jnp.arange(S // Q_BLOCK))
    return jnp.moveaxis(out, 0, 1).reshape(B, S, HQ, D)


def even_mixer(x, w_in, gate_w2, gate_b, gla_norm_g, fox_gate_b, w_out):
    B, S, _ = x.shape
    gq, gk, gv, gr, glr, fq, fk, fv, ff = _split(x @ w_in, _even_sizes())
    log_a = jax.nn.log_sigmoid((glr @ gate_w2).astype(F32) + gate_b.astype(F32)) / GLA_GATE_TAU
    hk = lambda t, d: t.reshape(B, S, -1, d).astype(F32)
    o_a = gla_chunked(hk(gq, GLA_DK), hk(gk, GLA_DK), hk(gv, GLA_DV), hk(log_a, GLA_DK))
    o_a = rms_norm(o_a, gla_norm_g) * jax.nn.silu(hk(gr, GLA_DV))
    log_f = jax.nn.log_sigmoid(ff.astype(F32) + fox_gate_b.astype(F32))
    o_b = fox_attention(hk(fq, FOX_DH), hk(fk, FOX_DH), hk(fv, FOX_DH), log_f)
    o = jnp.concatenate([o_a.reshape(B, S, -1), o_b.reshape(B, S, -1)], axis=-1)
    return o.astype(x.dtype) @ w_out


def odd_mixer(x, w_in, w_out):
    B, S, _ = x.shape
    q, k, v, iq, ik, iw = _split(x @ w_in, _odd_sizes())
    pos = jnp.arange(S)
    q = rope(q.reshape(B, S, DSA_Q_HEADS, DSA_DH).astype(F32), pos)
    k = rope(k.reshape(B, S, DSA_KV_HEADS, DSA_DH).astype(F32), pos)
    v = v.reshape(B, S, DSA_KV_HEADS, DSA_DH).astype(F32)
    iq = rope(iq.reshape(B, S, IDX_HEADS, IDX_DH).astype(F32), pos)
    ik = rope(ik.reshape(B, S, 1, IDX_DH).astype(F32), pos)[:, :, 0]
    iw = iw.astype(F32) * IDX_HEADS ** -0.5
    o = dsa_attention(q, k, v, iq, ik, iw)
    return o.reshape(B, S, -1).astype(x.dtype) @ w_out


def hier_moe(x, wg, bg, we, be, w_gate, w_up, w_down):
    B, S, D = x.shape
    t = x.reshape(B * S, D)
    group_prob = jax.nn.softmax((t @ wg).astype(F32) + bg.astype(F32), axis=-1)
    g_prob, g_idx = lax.top_k(group_prob, TOP_GROUPS)
    expert_logits = jnp.einsum('td,gde->tge', t, we).astype(F32) + be.astype(F32)
    chosen = jnp.take_along_axis(expert_logits, g_idx[:, :, None], axis=1)[:, 0]
    e_val, e_idx = lax.top_k(chosen, TOP_EXPERTS)
    e_w = jax.nn.softmax(e_val, axis=-1) * g_prob
    within = jnp.sum(jax.nn.one_hot(e_idx, EXPERTS_PER_GROUP, dtype=F32) * e_w[..., None], axis=1)
    combine = jax.nn.one_hot(g_idx[:, 0], N_GROUPS, dtype=F32)[:, :, None] * within[:, None, :]
    out = jnp.zeros((B * S, D), F32)
    for g in range(N_GROUPS):
        h = jax.nn.silu(jnp.einsum('td,edf->tef', t, w_gate[g])) * jnp.einsum('td,edf->tef', t, w_up[g])
        h = h * combine[:, g, :, None].astype(h.dtype)
        out = out + jnp.einsum('tef,efd->td', h, w_down[g]).astype(F32)
    return out.reshape(B, S, D).astype(x.dtype)


def setup_inputs(seed: int = 0) -> dict:
    key = jax.random.key(seed)
    ks = iter(jax.random.split(key, 32))
    nrm = lambda shape, scale: jax.random.normal(next(ks), shape, F32) * scale
    ne, no = (DEPTH + 1) // 2, DEPTH // 2
    even_cols = int(sum(_even_sizes()))
    odd_cols = int(sum(_odd_sizes()))
    G, E, F, D = N_GROUPS, EXPERTS_PER_GROUP, EXPERT_FF, D_MODEL
    return {
        'x': nrm((BATCH, SEQ, D), 1.0),
        'a_w_in': nrm((ne, D, even_cols), D ** -0.5),
        'a_gla_gate_w2': nrm((ne, GLA_GATE_RANK, GLA_HEADS * GLA_DK), GLA_GATE_RANK ** -0.5),
        'a_gla_gate_b': nrm((ne, GLA_HEADS * GLA_DK), 0.1),
        'a_gla_norm_g': 1.0 + nrm((ne, GLA_DV), 0.02),
        'a_fox_gate_b': jax.random.uniform(next(ks), (ne, FOX_HEADS), F32, 1.0, 4.0),
        'a_w_out': nrm((ne, MIX_WIDTH, D), MIX_WIDTH ** -0.5 * DN_BETA),
        'c_w_in': nrm((no, D, odd_cols), D ** -0.5),
        'c_w_out': nrm((no, DSA_WIDTH, D), DSA_WIDTH ** -0.5 * DN_BETA),
        'ln_mix_g': 1.0 + nrm((DEPTH, D), 0.02),
        'ln_mix_b': nrm((DEPTH, D), 0.02),
        'ln_ffn_g': 1.0 + nrm((DEPTH, D), 0.02),
        'ln_ffn_b': nrm((DEPTH, D), 0.02),
        'moe_group_w': nrm((DEPTH, D, G), D ** -0.5),
        'moe_group_b': nrm((DEPTH, G), 0.01),
        'moe_expert_w': nrm((DEPTH, G, D, E), D ** -0.5),
        'moe_expert_b': nrm((DEPTH, G, E), 0.01),
        'moe_w_gate': nrm((DEPTH, G, E, D, F), D ** -0.5),
        'moe_w_up': nrm((DEPTH, G, E, D, F), D ** -0.5),
        'moe_w_down': nrm((DEPTH, G, E, F, D), F ** -0.5 * DN_BETA),
    }


def reference(x, a_w_in, a_gla_gate_w2, a_gla_gate_b, a_gla_norm_g, a_fox_gate_b, a_w_out,
              c_w_in, c_w_out, ln_mix_g, ln_mix_b, ln_ffn_g, ln_ffn_b,
              moe_group_w, moe_group_b, moe_expert_w, moe_expert_b,
              moe_w_gate, moe_w_up, moe_w_down):
    for layer in range(DEPTH):
        j = layer // 2
        if layer % 2 == 0:
            h = even_mixer(x, a_w_in[j], a_gla_gate_w2[j], a_gla_gate_b[j], a_gla_norm_g[j],
                           a_fox_gate_b[j], a_w_out[j])
        else:
            h = odd_mixer(x, c_w_in[j], c_w_out[j])
        x = layer_norm(DN_ALPHA * x + h, ln_mix_g[layer], ln_mix_b[layer])
        h = hier_moe(x, moe_group_w[layer], moe_group_b[layer], moe_expert_w[layer], moe_expert_b[layer],
                     moe_w_gate[layer], moe_w_up[layer], moe_w_down[layer])
        x = layer_norm(DN_ALPHA * x + h, ln_ffn_g[layer], ln_ffn_b[layer])
    return x
```

```python
import functools

import jax
import jax.numpy as jnp
import numpy as np
from jax import lax
from jax.experimental import pallas as pl
from jax.experimental.pallas import tpu as pltpu

F32 = jnp.float32
BF16 = jnp.bfloat16

DEPTH = 2
GLA_HEADS = 4
GLA_DK = 128
GLA_DV = 256
GLA_GATE_RANK = 16
GLA_GATE_TAU = 16.0
GLA_CHUNK = 64
FOX_HEADS = 8
FOX_DH = 128
DSA_Q_HEADS = 16
DSA_KV_HEADS = 4
DSA_DH = 128
IDX_HEADS = 16
IDX_DH = 64
DSA_TOPK_MAX = 256
N_GROUPS = 4
EXPERTS_PER_GROUP = 8
N_EXPERTS = N_GROUPS * EXPERTS_PER_GROUP
ROPE_THETA = 10000.0
LN_EPS = 1e-5
DN_ALPHA = (2 * DEPTH) ** 0.25

LANES = 128
VMEM_LIMIT_BYTES = 56 * 1024 * 1024

NEG_INF = float("-inf")


def _params(*semantics):
    return pltpu.CompilerParams(dimension_semantics=semantics, vmem_limit_bytes=VMEM_LIMIT_BYTES)


def _nt_dot(a, b):
    return lax.dot_general(a, b, (((1,), (1,)), ((), ())), preferred_element_type=F32)


def _tn_dot(a, b):
    return lax.dot_general(a, b, (((0,), (0,)), ((), ())), preferred_element_type=F32)


def _log_sigmoid(z):
    return jnp.minimum(z, 0.0) - jnp.log1p(jnp.exp(-jnp.abs(z)))


def _silu(z):
    return z * (1.0 / (1.0 + jnp.exp(-z)))


def _cumsum_rows(x):
    n = x.shape[0]
    row = lax.broadcasted_iota(jnp.int32, x.shape, 0)
    shift = 1
    while shift < n:
        x = x + jnp.where(row >= shift, pltpu.roll(x, shift, 0), 0.0)
        shift *= 2
    return x


def _rot_half(x, head_dim):
    half = head_dim // 2
    n = x.shape[-1]
    lane = lax.broadcasted_iota(jnp.int32, x.shape, 1)
    fwd = pltpu.roll(x, n - half, 1)
    bwd = pltpu.roll(x, half, 1)
    return jnp.where((lane % head_dim) < half, fwd, bwd)


def _layer_norm(y, g, b):
    mu = jnp.mean(y, axis=-1, keepdims=True)
    d = y - mu
    var = jnp.mean(d * d, axis=-1, keepdims=True)
    return d * lax.rsqrt(var + LN_EPS) * g + b


def _proj_kernel(*refs, rope_dim, rope_lanes):
    if rope_dim:
        x_ref, w_ref, cos_ref, sin_ref, o_ref = refs
    else:
        x_ref, w_ref, o_ref = refs
    acc = jnp.dot(x_ref[...].astype(BF16), w_ref[...].astype(BF16), preferred_element_type=F32)
    if rope_dim:
        reps = acc.shape[-1] // LANES
        cos = jnp.tile(cos_ref[...], (1, reps))
        sin = jnp.tile(sin_ref[...], (1, reps))
        roped = acc * cos + _rot_half(acc, rope_dim) * sin
        if rope_lanes < acc.shape[-1]:
            lane = lax.broadcasted_iota(jnp.int32, acc.shape, 1)
            roped = jnp.where(lane < rope_lanes, roped, acc)
        acc = roped
    o_ref[...] = acc.astype(o_ref.dtype)


def _project(x, w, col_start, n_cols, out_dtype, seq_len, rope=None, rope_lanes=None, tm=512, tn=512):
    t, k = x.shape
    tn = min(tn, n_cols)
    assert t % tm == 0 and n_cols % tn == 0 and col_start % tn == 0 and seq_len % tm == 0
    col_off = col_start // tn
    in_specs = [
        pl.BlockSpec((tm, k), lambda j, i: (i, 0)),
        pl.BlockSpec((k, tn), lambda j, i: (0, j + col_off)),
    ]
    args = [x, w]
    rope_dim = 0
    if rope is not None:
        rope_dim, cos, sin = rope
        seq_tiles = seq_len // tm
        in_specs += [pl.BlockSpec((tm, LANES), lambda j, i: (i % seq_tiles, 0))] * 2
        args += [cos, sin]
    return pl.pallas_call(
        functools.partial(_proj_kernel, rope_dim=rope_dim, rope_lanes=rope_lanes or n_cols),
        grid=(n_cols // tn, t // tm),
        in_specs=in_specs,
        out_specs=pl.BlockSpec((tm, tn), lambda j, i: (i, j)),
        out_shape=jax.ShapeDtypeStruct((t, n_cols), out_dtype),
        compiler_params=_params("parallel", "parallel"),
        name="in_proj",
    )(*args)


def _rope_tables(seq_len, head_dim):
    half = head_dim // 2
    inv = ROPE_THETA ** (-jnp.arange(half, dtype=F32) / half)
    ang = jnp.arange(seq_len, dtype=F32)[:, None] * inv[None, :]
    cos, sin = jnp.cos(ang), jnp.sin(ang)
    reps = LANES // head_dim
    cos_full = jnp.tile(jnp.concatenate([cos, cos], axis=-1), (1, reps))
    sin_signed = jnp.tile(jnp.concatenate([-sin, sin], axis=-1), (1, reps))
    return cos_full, sin_signed


def _out_proj_ln_kernel(oa_ref, ob_ref, w_ref, x_ref, g_ref, b_ref, out_ref, acc_ref, *, n_k):
    k = pl.program_id(1)

    @pl.when(k == 0)
    def _():
        acc_ref[...] = jnp.zeros_like(acc_ref)

    @pl.when(k < n_k // 2)
    def _():
        acc_ref[...] += jnp.dot(oa_ref[...], w_ref[...].astype(BF16), preferred_element_type=F32)

    @pl.when(k >= n_k // 2)
    def _():
        acc_ref[...] += jnp.dot(ob_ref[...], w_ref[...].astype(BF16), preferred_element_type=F32)

    @pl.when(k == n_k - 1)
    def _():
        y = DN_ALPHA * x_ref[...] + acc_ref[...]
        out_ref[...] = _layer_norm(y, g_ref[...], b_ref[...])


def _out_proj_ln(o_a, o_b, b_col_start, w, x, g, b, tm=512, tk=512):
    t = o_a.shape[0]
    kdim, d = w.shape
    n_k = kdim // tk
    half = n_k // 2
    b_off = b_col_start // tk
    return pl.pallas_call(
        functools.partial(_out_proj_ln_kernel, n_k=n_k),
        grid=(t // tm, n_k),
        in_specs=[
            pl.BlockSpec((tm, tk), lambda i, k: (i, jnp.minimum(k, half - 1))),
            pl.BlockSpec((tm, tk), lambda i, k: (i, jnp.maximum(k - half, 0) + b_off)),
            pl.BlockSpec((tk, d), lambda i, k: (k, 0)),
            pl.BlockSpec((tm, d), lambda i, k: (i, 0)),
            pl.BlockSpec((1, d), lambda i, k: (0, 0)),
            pl.BlockSpec((1, d), lambda i, k: (0, 0)),
        ],
        out_specs=pl.BlockSpec((tm, d), lambda i, k: (i, 0)),
        out_shape=jax.ShapeDtypeStruct((t, d), F32),
        scratch_shapes=[pltpu.VMEM((tm, d), F32)],
        compiler_params=_params("parallel", "arbitrary"),
        name="out_proj_ln",
    )(o_a, o_b, w, x, g.reshape(1, d), b.reshape(1, d))


def _gla_kernel(q_ref, k_ref, v_ref, r_ref, sm_ref, w2_ref, gb_ref, ng_ref, o_ref, state_ref, *, n_chunks):
    c_len, dk, dv = GLA_CHUNK, GLA_DK, GLA_DV

    @pl.when(pl.program_id(1) == 0)
    def _():
        state_ref[...] = jnp.zeros_like(state_ref)

    row = lax.broadcasted_iota(jnp.int32, (c_len, c_len), 0)
    col = lax.broadcasted_iota(jnp.int32, (c_len, c_len), 1)
    causal = row >= col
    norm_g = ng_ref[...]
    for c in range(n_chunks):
        rows = slice(c * c_len, (c + 1) * c_len)
        glr = sm_ref[rows, FOX_HEADS:FOX_HEADS + GLA_GATE_RANK].astype(BF16)
        for h in range(GLA_HEADS):
            kcols = slice(h * dk, (h + 1) * dk)
            vcols = slice(h * dv, (h + 1) * dv)
            z = jnp.dot(glr, w2_ref[:, kcols].astype(BF16), preferred_element_type=F32) + gb_ref[:, kcols]
            log_a = _log_sigmoid(z) / GLA_GATE_TAU
            b = _cumsum_rows(log_a)
            b_last = b[c_len - 1:c_len, :]
            q = q_ref[rows, kcols] * (dk ** -0.5)
            k = k_ref[rows, kcols]
            v = v_ref[rows, vcols].astype(BF16)
            q_dec = (q * jnp.exp(b)).astype(BF16)
            k_inv = (k * jnp.exp(-b)).astype(BF16)
            k_end = (k * jnp.exp(b_last - b)).astype(BF16)
            scores = jnp.where(causal, _nt_dot(q_dec, k_inv), 0.0)
            state_t = state_ref[h]
            o = jnp.dot(scores.astype(BF16), v, preferred_element_type=F32)
            o = o + _nt_dot(q_dec, state_t.astype(BF16))
            state_ref[h] = jnp.exp(b_last) * state_t + _tn_dot(v, k_end)
            rms = lax.rsqrt(jnp.mean(o * o, axis=-1, keepdims=True) + LN_EPS)
            o_ref[rows, vcols] = (o * rms * norm_g * _silu(r_ref[rows, vcols])).astype(o_ref.dtype)


def _gla(main, small, gate_w2, gate_b, norm_g, batch, seq_len, rows_per_step=256):
    t = main.shape[0]
    n_steps = seq_len // rows_per_step
    hk, hv = GLA_HEADS * GLA_DK, GLA_HEADS * GLA_DV
    row_map = lambda b, n: b * n_steps + n
    return pl.pallas_call(
        functools.partial(_gla_kernel, n_chunks=rows_per_step // GLA_CHUNK),
        grid=(batch, n_steps),
        in_specs=[
            pl.BlockSpec((rows_per_step, hk), lambda b, n: (row_map(b, n), 0)),
            pl.BlockSpec((rows_per_step, hk), lambda b, n: (row_map(b, n), 1)),
            pl.BlockSpec((rows_per_step, hv), lambda b, n: (row_map(b, n), 1)),
            pl.BlockSpec((rows_per_step, hv), lambda b, n: (row_map(b, n), 2)),
            pl.BlockSpec((rows_per_step, LANES), lambda b, n: (row_map(b, n), 0)),
            pl.BlockSpec((GLA_GATE_RANK, hk), lambda b, n: (0, 0)),
            pl.BlockSpec((1, hk), lambda b, n: (0, 0)),
            pl.BlockSpec((1, GLA_DV), lambda b, n: (0, 0)),
        ],
        out_specs=pl.BlockSpec((rows_per_step, hv), lambda b, n: (row_map(b, n), 0)),
        out_shape=jax.ShapeDtypeStruct((t, hv), BF16),
        scratch_shapes=[pltpu.VMEM((GLA_HEADS, GLA_DV, GLA_DK), F32)],
        compiler_params=_params("parallel", "arbitrary"),
        name="gla",
    )(main, main, main, main, small, gate_w2, gate_b.reshape(1, hk), norm_g.reshape(1, GLA_DV))


def _fox_gate_kernel(sm_ref, b_ref, c_ref, ct_ref):
    c = _cumsum_rows(_log_sigmoid(sm_ref[...] + b_ref[...]))
    c_ref[...] = c
    ct_ref[0] = c.T


def _fox_gate(small, gate_b_row, batch, seq_len):
    t = small.shape[0]
    return pl.pallas_call(
        _fox_gate_kernel,
        grid=(batch,),
        in_specs=[pl.BlockSpec((seq_len, LANES), lambda b: (b, 0)), pl.BlockSpec((1, LANES), lambda b: (0, 0))],
        out_specs=[pl.BlockSpec((seq_len, LANES), lambda b: (b, 0)),
                   pl.BlockSpec((1, LANES, seq_len), lambda b: (b, 0, 0))],
        out_shape=[jax.ShapeDtypeStruct((t, LANES), F32), jax.ShapeDtypeStruct((batch, LANES, seq_len), F32)],
        compiler_params=_params("parallel"),
        name="fox_gate",
    )(small, gate_b_row)


def _fox_kernel(q_ref, k_ref, v_ref, cq_ref, ck_ref, o_ref, *, tq):
    qi = pl.program_id(2)
    q = q_ref[...]
    cq = cq_ref[0]
    scale = FOX_DH ** -0.5

    def step(j, carry, masked):
        m, l, acc = carry
        ks = pl.multiple_of(j * tq, tq)
        k = k_ref[pl.ds(ks, tq), :]
        v = v_ref[pl.ds(ks, tq), :]
        s = _nt_dot(q, k) * scale + (cq - ck_ref[0, :, pl.ds(ks, tq)])
        if masked:
            row = lax.broadcasted_iota(jnp.int32, s.shape, 0)
            col = lax.broadcasted_iota(jnp.int32, s.shape, 1)
            s = jnp.where(col <= row, s, NEG_INF)
        m_new = jnp.maximum(m, jnp.max(s, axis=-1, keepdims=True))
        p = jnp.exp(s - m_new)
        alpha = jnp.exp(m - m_new)
        l = alpha * l + jnp.sum(p, axis=-1, keepdims=True)
        acc = alpha * acc + jnp.dot(p.astype(BF16), v, preferred_element_type=F32)
        return m_new, l, acc

    init = (jnp.full((tq, 1), -1e30, F32), jnp.zeros((tq, 1), F32), jnp.zeros((tq, FOX_DH), F32))
    carry = lax.fori_loop(0, qi, functools.partial(step, masked=False), init)
    _, l, acc = step(qi, carry, True)
    o_ref[...] = (acc / l).astype(o_ref.dtype)


def _fox(fqkv, cq, ck, batch, seq_len, tq=256):
    t = fqkv.shape[0]
    nq = seq_len // tq
    h_cnt = FOX_HEADS
    return pl.pallas_call(
        functools.partial(_fox_kernel, tq=tq),
        grid=(batch, h_cnt, nq),
        in_specs=[
            pl.BlockSpec((tq, FOX_DH), lambda b, h, i: (b * nq + i, h)),
            pl.BlockSpec((seq_len, FOX_DH), lambda b, h, i: (b, h_cnt + h)),
            pl.BlockSpec((seq_len, FOX_DH), lambda b, h, i: (b, 2 * h_cnt + h)),
            pl.BlockSpec((1, tq, 1), lambda b, h, i: (b * h_cnt + h, i, 0)),
            pl.BlockSpec((1, 1, seq_len), lambda b, h, i: (b * h_cnt + h, 0, 0)),
        ],
        out_specs=pl.BlockSpec((tq, FOX_DH), lambda b, h, i: (b * nq + i, h)),
        out_shape=jax.ShapeDtypeStruct((t, h_cnt * FOX_DH), BF16),
        compiler_params=_params("parallel", "parallel", "arbitrary"),
        name="fox_attn",
    )(fqkv, fqkv, fqkv, cq, ck)


def _dsa_kernel(q_ref, k_ref, v_ref, iq_ref, ik_ref, iw_ref, o_ref, sc_ref, m_ref, l_ref, acc_ref,
                *, tq, tk, top_k, max_iters):
    qi = pl.program_id(1)
    n_kc = (qi * tq + tq + tk - 1) // tk
    group = DSA_Q_HEADS // DSA_KV_HEADS
    qpos = qi * tq + lax.broadcasted_iota(jnp.int32, (tq, 1), 0)

    iq = iq_ref[0].reshape(IDX_HEADS * tq, IDX_DH)
    iw = iw_ref[:, IDX_DH:IDX_DH + IDX_HEADS] * (IDX_HEADS ** -0.5 * IDX_DH ** -0.5)

    def score_chunk(c, carry):
        lo, hi = carry
        ks = pl.multiple_of(c * tk, tk)
        ik = ik_ref[pl.ds(ks, tk), 0:IDX_DH].astype(BF16)
        rel = _nt_dot(iq, ik)
        score = jnp.zeros((tq, tk), F32)
        for h in range(IDX_HEADS):
            score = score + iw[:, h:h + 1] * jnp.maximum(rel[h * tq:(h + 1) * tq, :], 0.0)
        kpos = ks + lax.broadcasted_iota(jnp.int32, (tq, tk), 1)
        valid = kpos <= qpos
        sc_ref[:, pl.ds(ks, tk)] = jnp.where(valid, score, NEG_INF)
        lo = jnp.minimum(lo, jnp.min(jnp.where(valid, score, jnp.inf), axis=-1, keepdims=True))
        hi = jnp.maximum(hi, jnp.max(jnp.where(valid, score, NEG_INF), axis=-1, keepdims=True))
        return lo, hi

    lo0, hi0 = lax.fori_loop(0, n_kc, score_chunk,
                             (jnp.full((tq, 1), jnp.inf, F32), jnp.full((tq, 1), NEG_INF, F32)))

    def count_ge(thr):
        def body(c, acc):
            ks = pl.multiple_of(c * tk, tk)
            hit = (sc_ref[:, pl.ds(ks, tk)] >= thr).astype(F32)
            return acc + jnp.sum(hit, axis=-1, keepdims=True)
        return lax.fori_loop(0, n_kc, body, jnp.zeros((tq, 1), F32))

    k_f = float(top_k)
    n_keys = (qpos + 1).astype(F32)

    def unsettled(cnt_lo):
        return jnp.max(jnp.where(cnt_lo > k_f, 1.0, 0.0)) > 0.0

    def cond(state):
        it, _, _, cnt_lo = state
        return jnp.logical_and(it < max_iters, unsettled(cnt_lo))

    def refine(state):
        it, lo, hi, cnt_lo = state
        mid = 0.5 * (lo + hi)
        cnt = count_ge(mid)
        keep = cnt >= k_f
        lo = jnp.where(keep, mid, lo)
        cnt_lo = jnp.where(keep, cnt, cnt_lo)
        hi = jnp.where(keep, hi, mid)
        return it + 1, lo, hi, cnt_lo

    _, thr, _, _ = lax.while_loop(cond, refine, (jnp.int32(0), lo0, hi0, n_keys))

    m_ref[...] = jnp.full_like(m_ref, NEG_INF)
    l_ref[...] = jnp.zeros_like(l_ref)
    acc_ref[...] = jnp.zeros_like(acc_ref)
    scale = DSA_DH ** -0.5

    def attend(c, carry):
        ks = pl.multiple_of(c * tk, tk)
        sel = sc_ref[:, pl.ds(ks, tk)] >= thr
        for g in range(DSA_KV_HEADS):
            heads = slice(g * group, (g + 1) * group)
            qg = jnp.concatenate(
                [q_ref[:, (g * group + j) * DSA_DH:(g * group + j + 1) * DSA_DH] for j in range(group)], axis=0)
            kc = k_ref[pl.ds(ks, tk), g * DSA_DH:(g + 1) * DSA_DH]
            vc = v_ref[pl.ds(ks, tk), g * DSA_DH:(g + 1) * DSA_DH]
            s = (_nt_dot(qg, kc) * scale).reshape(group, tq, tk)
            s = jnp.where(sel[None], s, NEG_INF)
            m_old = m_ref[heads]
            m_new = jnp.maximum(m_old, jnp.max(s, axis=-1, keepdims=True))
            m_safe = jnp.where(m_new == NEG_INF, 0.0, m_new)
            p = jnp.exp(s - m_safe)
            alpha = jnp.exp(m_old - m_safe)
            l_ref[heads] = alpha * l_ref[heads] + jnp.sum(p, axis=-1, keepdims=True)
            pv = jnp.dot(p.reshape(group * tq, tk).astype(BF16), vc, preferred_element_type=F32)
            acc_ref[heads] = alpha * acc_ref[heads] + pv.reshape(group, tq, DSA_DH)
            m_ref[heads] = m_new
        return carry

    lax.fori_loop(0, n_kc, attend, 0)
    for h in range(DSA_Q_HEADS):
        o_ref[:, h * DSA_DH:(h + 1) * DSA_DH] = (acc_ref[h] / l_ref[h]).astype(o_ref.dtype)


def _dsa(qk, v, iq_hm, small, batch, seq_len, tq=128, tk=512):
    t = qk.shape[0]
    tk = min(tk, seq_len)
    nq = seq_len // tq
    qw = DSA_Q_HEADS * DSA_DH
    kw = DSA_KV_HEADS * DSA_DH
    top_k = min(DSA_TOPK_MAX, seq_len // 4)
    return pl.pallas_call(
        functools.partial(_dsa_kernel, tq=tq, tk=tk, top_k=top_k, max_iters=64),
        grid=(batch, nq),
        in_specs=[
            pl.BlockSpec((tq, qw), lambda b, i: (b * nq + i, 0)),
            pl.BlockSpec((seq_len, kw), lambda b, i: (b, qw // kw)),
            pl.BlockSpec((seq_len, kw), lambda b, i: (b, 0)),
            pl.BlockSpec((1, IDX_HEADS, tq, IDX_DH), lambda b, i: (b, 0, i, 0)),
            pl.BlockSpec((seq_len, LANES), lambda b, i: (b, 0)),
            pl.BlockSpec((tq, LANES), lambda b, i: (b * nq + i, 0)),
        ],
        out_specs=pl.BlockSpec((tq, qw), lambda b, i: (b * nq + i, 0)),
        out_shape=jax.ShapeDtypeStruct((t, qw), BF16),
        scratch_shapes=[
            pltpu.VMEM((tq, seq_len), F32),
            pltpu.VMEM((DSA_Q_HEADS, tq, 1), F32),
            pltpu.VMEM((DSA_Q_HEADS, tq, 1), F32),
            pltpu.VMEM((DSA_Q_HEADS, tq, DSA_DH), F32),
        ],
        compiler_params=_params("parallel", "arbitrary"),
        name="dsa",
    )(qk, qk, v, iq_hm, small, small)


def _router_kernel(x_ref, w_ref, b_ref, o_ref):
    logits = jnp.dot(x_ref[...], w_ref[...], preferred_element_type=F32,
                     precision=lax.Precision.HIGHEST) + b_ref[...]
    lane = lax.broadcasted_iota(jnp.int32, logits.shape, 1)
    big = jnp.int32(LANES)
    g_logits = jnp.where(lane < N_GROUPS, logits, NEG_INF)
    g_max = jnp.max(g_logits, axis=-1, keepdims=True)
    g_prob = 1.0 / jnp.sum(jnp.exp(g_logits - g_max), axis=-1, keepdims=True)
    g_idx = jnp.min(jnp.where(g_logits == g_max, lane, big), axis=-1, keepdims=True)
    first = N_GROUPS + g_idx * EXPERTS_PER_GROUP
    e_logits = jnp.where((lane >= first) & (lane < first + EXPERTS_PER_GROUP), logits, NEG_INF)
    v1 = jnp.max(e_logits, axis=-1, keepdims=True)
    i1 = jnp.min(jnp.where(e_logits == v1, lane, big), axis=-1, keepdims=True)
    rest = jnp.where(lane == i1, NEG_INF, e_logits)
    v2 = jnp.max(rest, axis=-1, keepdims=True)
    i2 = jnp.min(jnp.where(rest == v2, lane, big), axis=-1, keepdims=True)
    e2 = jnp.exp(v2 - v1)
    w1 = (1.0 / (1.0 + e2)) * g_prob
    w2 = (e2 / (1.0 + e2)) * g_prob
    out = jnp.where(lane == 0, (i1 - N_GROUPS).astype(F32), 0.0)
    out = jnp.where(lane == 1, (i2 - N_GROUPS).astype(F32), out)
    out = jnp.where(lane == 2, w1, out)
    out = jnp.where(lane == 3, w2, out)
    o_ref[...] = out


def _router(x, w_cat, b_cat, tm=512):
    t, d = x.shape
    return pl.pallas_call(
        _router_kernel,
        grid=(t // tm,),
        in_specs=[pl.BlockSpec((tm, d), lambda i: (i, 0)),
                  pl.BlockSpec((d, LANES), lambda i: (0, 0)),
                  pl.BlockSpec((1, LANES), lambda i: (0, 0))],
        out_specs=pl.BlockSpec((tm, LANES), lambda i: (i, 0)),
        out_shape=jax.ShapeDtypeStruct((t, LANES), F32),
        compiler_params=_params("parallel"),
        name="moe_router",
    )(x, w_cat, b_cat)


def _experts_kernel(tile_e_ref, tile_start_ref, tile_n_ref, src_ref, dst_ref,
                    x_hbm, wg_ref, wu_ref, wd_ref, y_hbm, xbuf, ybuf, gsem, ssem, *, tm, n_tiles):
    del tile_e_ref
    t = pl.program_id(0)
    slot = t % 2

    def gather_copy(tile, s, i):
        row = src_ref[tile_start_ref[tile] + i]
        return pltpu.make_async_copy(x_hbm.at[pl.ds(row, 1)], xbuf.at[s, pl.ds(i, 1)], gsem.at[s])

    def scatter_copy(tile, i):
        row = dst_ref[tile_start_ref[tile] + i]
        return pltpu.make_async_copy(ybuf.at[pl.ds(i, 1)], y_hbm.at[pl.ds(row, 1)], ssem.at[0])

    def start_gather(tile, s):
        def body(i, c):
            gather_copy(tile, s, i).start()
            return c
        lax.fori_loop(0, tile_n_ref[tile], body, 0)

    def wait_gather(tile, s):
        def body(i, c):
            gather_copy(tile, s, i).wait()
            return c
        lax.fori_loop(0, tile_n_ref[tile], body, 0)

    @pl.when(t == 0)
    def _():
        xbuf[...] = jnp.zeros_like(xbuf)
        start_gather(0, 0)

    @pl.when(t + 1 < n_tiles)
    def _():
        start_gather(t + 1, 1 - slot)

    n_rows = tile_n_ref[t]

    @pl.when(n_rows > 0)
    def _():
        wait_gather(t, slot)
        xb = xbuf[slot].astype(BF16)
        gate = jnp.dot(xb, wg_ref[0, 0, 0].astype(BF16), preferred_element_type=F32)
        up = jnp.dot(xb, wu_ref[0, 0, 0].astype(BF16), preferred_element_type=F32)
        hidden = (_silu(gate) * up).astype(BF16)
        ybuf[...] = jnp.dot(hidden, wd_ref[0, 0, 0].astype(BF16), preferred_element_type=F32)

        def s_start(i, c):
            scatter_copy(t, i).start()
            return c
        lax.fori_loop(0, n_rows, s_start, 0)

        def s_wait(i, c):
            scatter_copy(t, i).wait()
            return c
        lax.fori_loop(0, n_rows, s_wait, 0)


def _experts(x, w_gate, w_up, w_down, layer, tile_e, tile_start, tile_n, src_rows, dst_rows, tm, n_tiles):
    t_tokens, d = x.shape
    ff = w_gate.shape[-1]
    w_map = lambda t, te, ts, tn, sr, ds_: (layer, te[t] // EXPERTS_PER_GROUP, te[t] % EXPERTS_PER_GROUP, 0, 0)
    grid_spec = pltpu.PrefetchScalarGridSpec(
        num_scalar_prefetch=5,
        grid=(n_tiles,),
        in_specs=[
            pl.BlockSpec(memory_space=pl.ANY),
            pl.BlockSpec((1, 1, 1, d, ff), w_map),
            pl.BlockSpec((1, 1, 1, d, ff), w_map),
            pl.BlockSpec((1, 1, 1, ff, d), w_map),
        ],
        out_specs=pl.BlockSpec(memory_space=pl.ANY),
        scratch_shapes=[
            pltpu.VMEM((2, tm, d), F32),
            pltpu.VMEM((tm, d), F32),
            pltpu.SemaphoreType.DMA((2,)),
            pltpu.SemaphoreType.DMA((1,)),
        ],
    )
    return pl.pallas_call(
        functools.partial(_experts_kernel, tm=tm, n_tiles=n_tiles),
        grid_spec=grid_spec,
        out_shape=jax.ShapeDtypeStruct((2 * t_tokens, d), F32),
        compiler_params=_params("arbitrary"),
        name="moe_experts",
    )(tile_e, tile_start, tile_n, src_rows, dst_rows, x, w_gate, w_up, w_down)


def _combine_ln_kernel(x_ref, y0_ref, y1_ref, r_ref, g_ref, b_ref, o_ref):
    w0 = r_ref[:, 2:3]
    w1 = r_ref[:, 3:4]
    y = DN_ALPHA * x_ref[...] + (w0 * y0_ref[...] + w1 * y1_ref[...])
    o_ref[...] = _layer_norm(y, g_ref[...], b_ref[...])


def _combine_ln(x, y2, route, g, b, tm=512):
    t, d = x.shape
    n_t = t // tm
    return pl.pallas_call(
        _combine_ln_kernel,
        grid=(n_t,),
        in_specs=[
            pl.BlockSpec((tm, d), lambda i: (i, 0)),
            pl.BlockSpec((tm, d), lambda i: (i, 0)),
            pl.BlockSpec((tm, d), lambda i: (i + n_t, 0)),
            pl.BlockSpec((tm, LANES), lambda i: (i, 0)),
            pl.BlockSpec((1, d), lambda i: (0, 0)),
            pl.BlockSpec((1, d), lambda i: (0, 0)),
        ],
        out_specs=pl.BlockSpec((tm, d), lambda i: (i, 0)),
        out_shape=jax.ShapeDtypeStruct((t, d), F32),
        compiler_params=_params("parallel"),
        name="moe_combine_ln",
    )(x, y2, y2, route, g.reshape(1, d), b.reshape(1, d))


def _moe_layer(x, wg, bg, we, be, w_gate, w_up, w_down, layer, ln_g, ln_b, tm=256):
    t, d = x.shape
    w_cat = jnp.concatenate([wg, jnp.transpose(we, (1, 0, 2)).reshape(d, N_EXPERTS)], axis=1)
    w_cat = jnp.pad(w_cat, ((0, 0), (0, LANES - w_cat.shape[1])))
    b_cat = jnp.pad(jnp.concatenate([bg, be.reshape(-1)]), (0, LANES - N_GROUPS - N_EXPERTS)).reshape(1, LANES)
    route = _router(x, w_cat, b_cat)

    expert = jnp.concatenate([route[:, 0], route[:, 1]]).astype(jnp.int32)
    order = jnp.argsort(expert, stable=True).astype(jnp.int32)
    counts = jnp.sum(expert[:, None] == jnp.arange(N_EXPERTS, dtype=jnp.int32)[None, :], axis=0, dtype=jnp.int32)
    seg_start = jnp.cumsum(counts) - counts
    tiles_per = (counts + tm - 1) // tm
    tile_end = jnp.cumsum(tiles_per)
    tile_begin = tile_end - tiles_per
    n_tiles = (2 * t) // tm + N_EXPERTS
    tile_ids = jnp.arange(n_tiles, dtype=jnp.int32)
    last_used = jnp.maximum(tile_end[-1] - 1, 0)
    tile_e = jnp.searchsorted(tile_end, jnp.minimum(tile_ids, last_used), side="right").astype(jnp.int32)
    tile_e = jnp.minimum(tile_e, N_EXPERTS - 1)
    within = (tile_ids - tile_begin[tile_e]) * tm
    tile_n = jnp.where(tile_ids < tile_end[-1], jnp.clip(counts[tile_e] - within, 0, tm), 0).astype(jnp.int32)
    tile_start = jnp.where(tile_n > 0, seg_start[tile_e] + within, 0).astype(jnp.int32)
    src_rows = order % t
    y2 = _experts(x, w_gate, w_up, w_down, layer, tile_e, tile_start, tile_n, src_rows, order, tm, n_tiles)
    return _combine_ln(x, y2, route, ln_g, ln_b)


def _even_mixer_layer(x, batch, seq_len, w_in, gate_w2, gate_b, norm_g, fox_gate_b, w_out, ln_g, ln_b):
    d = x.shape[1]
    gla_w = 2 * GLA_HEADS * GLA_DK + 2 * GLA_HEADS * GLA_DV
    fox_w = 3 * FOX_HEADS * FOX_DH
    glr0 = gla_w
    fox0 = glr0 + GLA_GATE_RANK
    ff0 = fox0 + fox_w
    main = _project(x, w_in, 0, gla_w, F32, seq_len)
    fqkv = _project(x, w_in[:, fox0:ff0], 0, fox_w, BF16, seq_len)
    w_small = jnp.concatenate([w_in[:, ff0:ff0 + FOX_HEADS], w_in[:, glr0:fox0]], axis=1)
    w_small = jnp.pad(w_small, ((0, 0), (0, LANES - w_small.shape[1])))
    small = _project(x, w_small, 0, LANES, F32, seq_len)
    o_gla = _gla(main, small, gate_w2, gate_b, norm_g, batch, seq_len)
    gate_b_row = jnp.pad(fox_gate_b, (0, LANES - FOX_HEADS)).reshape(1, LANES)
    c, c_t = _fox_gate(small, gate_b_row, batch, seq_len)
    cq = jnp.transpose(c.reshape(batch, seq_len, LANES)[:, :, :FOX_HEADS], (0, 2, 1))
    cq = cq.reshape(batch * FOX_HEADS, seq_len, 1)
    ck = c_t[:, :FOX_HEADS, :].reshape(batch * FOX_HEADS, 1, seq_len)
    o_fox = _fox(fqkv, cq, ck, batch, seq_len)
    return _out_proj_ln(o_gla, o_fox, 0, w_out, x, ln_g, ln_b)


def _odd_mixer_layer(x, batch, seq_len, w_in, w_out, ln_g, ln_b):
    qw = DSA_Q_HEADS * DSA_DH
    kw = DSA_KV_HEADS * DSA_DH
    iqw = IDX_HEADS * IDX_DH
    rope_attn = (DSA_DH,) + _rope_tables(seq_len, DSA_DH)
    rope_idx = (IDX_DH,) + _rope_tables(seq_len, IDX_DH)
    qk = _project(x, w_in, 0, qw + kw, BF16, seq_len, rope=rope_attn)
    v = _project(x, w_in, qw + kw, kw, BF16, seq_len)
    iq = _project(x, w_in, qw + 2 * kw, iqw, BF16, seq_len, rope=rope_idx)
    small0 = qw + 2 * kw + iqw
    w_small = jnp.pad(w_in[:, small0:], ((0, 0), (0, LANES - (w_in.shape[1] - small0))))
    small = _project(x, w_small, 0, LANES, F32, seq_len, rope=rope_idx, rope_lanes=IDX_DH)
    iq_hm = jnp.transpose(iq.reshape(batch, seq_len, IDX_HEADS, IDX_DH), (0, 2, 1, 3))
    o = _dsa(qk, v, iq_hm, small, batch, seq_len)
    return _out_proj_ln(o, o, qw // 2, w_out, x, ln_g, ln_b)


def kernel(x, a_w_in, a_gla_gate_w2, a_gla_gate_b, a_gla_norm_g, a_fox_gate_b, a_w_out, c_w_in, c_w_out,
           ln_mix_g, ln_mix_b, ln_ffn_g, ln_ffn_b, moe_group_w, moe_group_b, moe_expert_w, moe_expert_b,
           moe_w_gate, moe_w_up, moe_w_down):
    batch, seq_len, d = x.shape
    h = x.reshape(batch * seq_len, d)
    for layer in range(DEPTH):
        j = layer // 2
        if layer % 2 == 0:
            h = _even_mixer_layer(h, batch, seq_len, a_w_in[j], a_gla_gate_w2[j], a_gla_gate_b[j],
                                  a_gla_norm_g[j], a_fox_gate_b[j], a_w_out[j], ln_mix_g[layer], ln_mix_b[layer])
        else:
            h = _odd_mixer_layer(h, batch, seq_len, c_w_in[j], c_w_out[j], ln_mix_g[layer], ln_mix_b[layer])
        h = _moe_layer(h, moe_group_w[layer], moe_group_b[layer], moe_expert_w[layer], moe_expert_b[layer],
                       moe_w_gate, moe_w_up, moe_w_down, layer, ln_ffn_g[layer], ln_ffn_b[layer])
    return h.reshape(batch, seq_len, d)
```

```python
import functools

import jax
import jax.numpy as jnp
import numpy as np
from jax import lax
from jax.experimental import pallas as pl
from jax.experimental.pallas import tpu as pltpu

F32 = jnp.float32
BF16 = jnp.bfloat16

DEPTH = 2
GLA_HEADS = 4
GLA_DK = 128
GLA_DV = 256
GLA_GATE_RANK = 16
GLA_GATE_TAU = 16.0
GLA_CHUNK = 64
FOX_HEADS = 8
FOX_DH = 128
DSA_Q_HEADS = 16
DSA_KV_HEADS = 4
DSA_DH = 128
IDX_HEADS = 16
IDX_DH = 64
DSA_TOPK_MAX = 256
N_GROUPS = 4
EXPERTS_PER_GROUP = 8
N_EXPERTS = N_GROUPS * EXPERTS_PER_GROUP
ROPE_THETA = 10000.0
LN_EPS = 1e-5
DN_ALPHA = (2 * DEPTH) ** 0.25

LANES = 128
SUBLANES = 8
VMEM_LIMIT_BYTES = 56 * 1024 * 1024

NEG_INF = float("-inf")
LOG2_E = 1.4426950408889634
MASK_BIAS = -1e30


def _params(*semantics):
    return pltpu.CompilerParams(dimension_semantics=semantics, vmem_limit_bytes=VMEM_LIMIT_BYTES)


def _nt_dot(a, b):
    return lax.dot_general(a, b, (((1,), (1,)), ((), ())), preferred_element_type=F32)


def _tn_dot(a, b):
    return lax.dot_general(a, b, (((0,), (0,)), ((), ())), preferred_element_type=F32)


def _log_sigmoid(z):
    return jnp.minimum(z, 0.0) - jnp.log1p(jnp.exp(-jnp.abs(z)))


def _silu(z):
    return z * (1.0 / (1.0 + jnp.exp(-z)))


def _cumsum_rows(x):
    n = x.shape[0]
    row = lax.broadcasted_iota(jnp.int32, x.shape, 0)
    shift = 1
    while shift < n:
        x = x + jnp.where(row >= shift, pltpu.roll(x, shift, 0), 0.0)
        shift *= 2
    return x


def _rot_half(x, head_dim):
    half = head_dim // 2
    n = x.shape[-1]
    lane = lax.broadcasted_iota(jnp.int32, x.shape, 1)
    fwd = pltpu.roll(x, n - half, 1)
    bwd = pltpu.roll(x, half, 1)
    return jnp.where((lane % head_dim) < half, fwd, bwd)


def _layer_norm(y, g, b):
    mu = jnp.mean(y, axis=-1, keepdims=True)
    d = y - mu
    var = jnp.mean(d * d, axis=-1, keepdims=True)
    return d * lax.rsqrt(var + LN_EPS) * g + b


def _proj_kernel(*refs, rope_dim, rope_lanes):
    if rope_dim:
        x_ref, w_ref, cos_ref, sin_ref, o_ref = refs
    else:
        x_ref, w_ref, o_ref = refs
    acc = jnp.dot(x_ref[...].astype(BF16), w_ref[...].astype(BF16), preferred_element_type=F32)
    if rope_dim:
        reps = acc.shape[-1] // LANES
        cos = jnp.tile(cos_ref[...], (1, reps))
        sin = jnp.tile(sin_ref[...], (1, reps))
        roped = acc * cos + _rot_half(acc, rope_dim) * sin
        if rope_lanes < acc.shape[-1]:
            lane = lax.broadcasted_iota(jnp.int32, acc.shape, 1)
            roped = jnp.where(lane < rope_lanes, roped, acc)
        acc = roped
    o_ref[...] = acc.astype(o_ref.dtype)


def _project(x, w, col_start, n_cols, out_dtype, seq_len, rope=None, rope_lanes=None, tm=512, tn=512):
    t, k = x.shape
    tn = min(tn, n_cols)
    assert t % tm == 0 and n_cols % tn == 0 and col_start % tn == 0 and seq_len % tm == 0
    col_off = col_start // tn
    in_specs = [
        pl.BlockSpec((tm, k), lambda j, i: (i, 0)),
        pl.BlockSpec((k, tn), lambda j, i: (0, j + col_off)),
    ]
    args = [x, w]
    rope_dim = 0
    if rope is not None:
        rope_dim, cos, sin = rope
        seq_tiles = seq_len // tm
        in_specs += [pl.BlockSpec((tm, LANES), lambda j, i: (i % seq_tiles, 0))] * 2
        args += [cos, sin]
    return pl.pallas_call(
        functools.partial(_proj_kernel, rope_dim=rope_dim, rope_lanes=rope_lanes or n_cols),
        grid=(n_cols // tn, t // tm),
        in_specs=in_specs,
        out_specs=pl.BlockSpec((tm, tn), lambda j, i: (i, j)),
        out_shape=jax.ShapeDtypeStruct((t, n_cols), out_dtype),
        compiler_params=_params("parallel", "parallel"),
        name="in_proj",
    )(*args)


def _rope_tables(seq_len, head_dim):
    half = head_dim // 2
    inv = ROPE_THETA ** (-jnp.arange(half, dtype=F32) / half)
    ang = jnp.arange(seq_len, dtype=F32)[:, None] * inv[None, :]
    cos, sin = jnp.cos(ang), jnp.sin(ang)
    reps = LANES // head_dim
    cos_full = jnp.tile(jnp.concatenate([cos, cos], axis=-1), (1, reps))
    sin_signed = jnp.tile(jnp.concatenate([-sin, sin], axis=-1), (1, reps))
    return cos_full, sin_signed


def _out_proj_ln_kernel(oa_ref, ob_ref, w_ref, x_ref, g_ref, b_ref, out_ref, acc_ref, *, n_k):
    k = pl.program_id(1)

    @pl.when(k == 0)
    def _():
        acc_ref[...] = jnp.zeros_like(acc_ref)

    @pl.when(k < n_k // 2)
    def _():
        acc_ref[...] += jnp.dot(oa_ref[...], w_ref[...].astype(BF16), preferred_element_type=F32)

    @pl.when(k >= n_k // 2)
    def _():
        acc_ref[...] += jnp.dot(ob_ref[...], w_ref[...].astype(BF16), preferred_element_type=F32)

    @pl.when(k == n_k - 1)
    def _():
        y = DN_ALPHA * x_ref[...] + acc_ref[...]
        out_ref[...] = _layer_norm(y, g_ref[...], b_ref[...])


def _out_proj_ln(o_a, o_b, b_col_start, w, x, g, b, tm=512, tk=512):
    t = o_a.shape[0]
    kdim, d = w.shape
    n_k = kdim // tk
    half = n_k // 2
    b_off = b_col_start // tk
    return pl.pallas_call(
        functools.partial(_out_proj_ln_kernel, n_k=n_k),
        grid=(t // tm, n_k),
        in_specs=[
            pl.BlockSpec((tm, tk), lambda i, k: (i, jnp.minimum(k, half - 1))),
            pl.BlockSpec((tm, tk), lambda i, k: (i, jnp.maximum(k - half, 0) + b_off)),
            pl.BlockSpec((tk, d), lambda i, k: (k, 0)),
            pl.BlockSpec((tm, d), lambda i, k: (i, 0)),
            pl.BlockSpec((1, d), lambda i, k: (0, 0)),
            pl.BlockSpec((1, d), lambda i, k: (0, 0)),
        ],
        out_specs=pl.BlockSpec((tm, d), lambda i, k: (i, 0)),
        out_shape=jax.ShapeDtypeStruct((t, d), F32),
        scratch_shapes=[pltpu.VMEM((tm, d), F32)],
        compiler_params=_params("parallel", "arbitrary"),
        name="out_proj_ln",
    )(o_a, o_b, w, x, g.reshape(1, d), b.reshape(1, d))


def _gla_kernel(q_ref, k_ref, v_ref, r_ref, sm_ref, w2_ref, gb_ref, ng_ref, o_ref, state_ref, *, n_chunks):
    c_len, dk, dv = GLA_CHUNK, GLA_DK, GLA_DV

    @pl.when(pl.program_id(1) == 0)
    def _():
        state_ref[...] = jnp.zeros_like(state_ref)

    row = lax.broadcasted_iota(jnp.int32, (c_len, c_len), 0)
    col = lax.broadcasted_iota(jnp.int32, (c_len, c_len), 1)
    causal = row >= col
    norm_g = ng_ref[...]
    for c in range(n_chunks):
        rows = slice(c * c_len, (c + 1) * c_len)
        glr = sm_ref[rows, FOX_HEADS:FOX_HEADS + GLA_GATE_RANK].astype(BF16)
        for h in range(GLA_HEADS):
            kcols = slice(h * dk, (h + 1) * dk)
            vcols = slice(h * dv, (h + 1) * dv)
            z = jnp.dot(glr, w2_ref[:, kcols].astype(BF16), preferred_element_type=F32) + gb_ref[:, kcols]
            log_a = _log_sigmoid(z) / GLA_GATE_TAU
            b = _cumsum_rows(log_a)
            b_last = b[c_len - 1:c_len, :]
            q = q_ref[rows, kcols] * (dk ** -0.5)
            k = k_ref[rows, kcols]
            v = v_ref[rows, vcols].astype(BF16)
            q_dec = (q * jnp.exp(b)).astype(BF16)
            k_inv = (k * jnp.exp(-b)).astype(BF16)
            k_end = (k * jnp.exp(b_last - b)).astype(BF16)
            scores = jnp.where(causal, _nt_dot(q_dec, k_inv), 0.0)
            state_t = state_ref[h]
            o = jnp.dot(scores.astype(BF16), v, preferred_element_type=F32)
            o = o + _nt_dot(q_dec, state_t.astype(BF16))
            state_ref[h] = jnp.exp(b_last) * state_t + _tn_dot(v, k_end)
            rms = lax.rsqrt(jnp.mean(o * o, axis=-1, keepdims=True) + LN_EPS)
            o_ref[rows, vcols] = (o * rms * norm_g * _silu(r_ref[rows, vcols])).astype(o_ref.dtype)


def _gla(main, small, gate_w2, gate_b, norm_g, batch, seq_len, rows_per_step=256):
    t = main.shape[0]
    n_steps = seq_len // rows_per_step
    hk, hv = GLA_HEADS * GLA_DK, GLA_HEADS * GLA_DV
    row_map = lambda b, n: b * n_steps + n
    return pl.pallas_call(
        functools.partial(_gla_kernel, n_chunks=rows_per_step // GLA_CHUNK),
        grid=(batch, n_steps),
        in_specs=[
            pl.BlockSpec((rows_per_step, hk), lambda b, n: (row_map(b, n), 0)),
            pl.BlockSpec((rows_per_step, hk), lambda b, n: (row_map(b, n), 1)),
            pl.BlockSpec((rows_per_step, hv), lambda b, n: (row_map(b, n), 1)),
            pl.BlockSpec((rows_per_step, hv), lambda b, n: (row_map(b, n), 2)),
            pl.BlockSpec((rows_per_step, LANES), lambda b, n: (row_map(b, n), 0)),
            pl.BlockSpec((GLA_GATE_RANK, hk), lambda b, n: (0, 0)),
            pl.BlockSpec((1, hk), lambda b, n: (0, 0)),
            pl.BlockSpec((1, GLA_DV), lambda b, n: (0, 0)),
        ],
        out_specs=pl.BlockSpec((rows_per_step, hv), lambda b, n: (row_map(b, n), 0)),
        out_shape=jax.ShapeDtypeStruct((t, hv), BF16),
        scratch_shapes=[pltpu.VMEM((GLA_HEADS, GLA_DV, GLA_DK), F32)],
        compiler_params=_params("parallel", "arbitrary"),
        name="gla",
    )(main, main, main, main, small, gate_w2, gate_b.reshape(1, hk), norm_g.reshape(1, GLA_DV))


def _fox_gate_kernel(sm_ref, b_ref, c_ref, ct_ref):
    c = _cumsum_rows(_log_sigmoid(sm_ref[...] + b_ref[...]))
    c_ref[...] = c
    ct_ref[0] = c.T


def _fox_gate(small, gate_b_row, batch, seq_len):
    t = small.shape[0]
    return pl.pallas_call(
        _fox_gate_kernel,
        grid=(batch,),
        in_specs=[pl.BlockSpec((seq_len, LANES), lambda b: (b, 0)), pl.BlockSpec((1, LANES), lambda b: (0, 0))],
        out_specs=[pl.BlockSpec((seq_len, LANES), lambda b: (b, 0)),
                   pl.BlockSpec((1, LANES, seq_len), lambda b: (b, 0, 0))],
        out_shape=[jax.ShapeDtypeStruct((t, LANES), F32), jax.ShapeDtypeStruct((batch, LANES, seq_len), F32)],
        compiler_params=_params("parallel"),
        name="fox_gate",
    )(small, gate_b_row)


def _fox_kernel(q_ref, k_ref, v_ref, cq_ref, ck_ref, o_ref, m_ref, l_ref, acc_ref, *, tq, heads):
    qi = pl.program_id(2)
    dh = FOX_DH
    scale = dh ** -0.5
    inv_scale = dh ** 0.5
    c_exp = scale * LOG2_E
    reps = tq // LANES

    m_ref[...] = jnp.full_like(m_ref, MASK_BIAS)
    l_ref[...] = jnp.zeros_like(l_ref)
    acc_ref[...] = jnp.zeros_like(acc_ref)

    def step(j, masked):
        ks = pl.multiple_of(j * tq, tq)
        for h in range(heads):
            cols = slice(h * dh, (h + 1) * dh)
            q = q_ref[:, cols]
            k = k_ref[pl.ds(ks, tq), cols]
            v = v_ref[pl.ds(ks, tq), cols]
            cq = cq_ref[0, 0, :, h:h + 1] * inv_scale
            s = _nt_dot(q, k) - ck_ref[h, :, pl.ds(ks, tq)] * inv_scale
            if masked:
                row = lax.broadcasted_iota(jnp.int32, s.shape, 0)
                col = lax.broadcasted_iota(jnp.int32, s.shape, 1)
                s = jnp.where(col <= row, s, MASK_BIAS)
            m_prev = m_ref[h]
            m_new = jnp.maximum(m_prev, jnp.max(s, axis=-1, keepdims=True) + cq)
            shift = m_new - cq
            p = jnp.exp2((s - jnp.tile(shift, (1, reps))) * c_exp)
            alpha = jnp.exp2((m_prev - m_new) * c_exp)
            p_sum = p[:, 0:LANES]
            for r in range(1, reps):
                p_sum = p_sum + p[:, r * LANES:(r + 1) * LANES]
            l_ref[h] = alpha * l_ref[h] + p_sum
            acc_ref[h] = alpha * acc_ref[h] + jnp.dot(p.astype(BF16), v, preferred_element_type=F32)
            m_ref[h] = m_new

    def body(j, carry):
        step(j, False)
        return carry

    lax.fori_loop(0, qi, body, 0)
    step(qi, True)
    for h in range(heads):
        l = jnp.sum(l_ref[h], axis=-1, keepdims=True)
        o_ref[:, h * dh:(h + 1) * dh] = (acc_ref[h] / l).astype(o_ref.dtype)


def _fox(fqkv, cq, ck, batch, seq_len, tq=512, heads=4):
    t = fqkv.shape[0]
    nq = seq_len // tq
    n_hg = FOX_HEADS // heads
    w = heads * FOX_DH
    return pl.pallas_call(
        functools.partial(_fox_kernel, tq=tq, heads=heads),
        grid=(batch, n_hg, nq),
        in_specs=[
            pl.BlockSpec((tq, w), lambda b, g, i: (b * nq + i, g)),
            pl.BlockSpec((seq_len, w), lambda b, g, i: (b, n_hg + g)),
            pl.BlockSpec((seq_len, w), lambda b, g, i: (b, 2 * n_hg + g)),
            pl.BlockSpec((1, 1, tq, heads), lambda b, g, i: (b, g, i, 0)),
            pl.BlockSpec((heads, 1, seq_len), lambda b, g, i: (b * n_hg + g, 0, 0)),
        ],
        out_specs=pl.BlockSpec((tq, w), lambda b, g, i: (b * nq + i, g)),
        out_shape=jax.ShapeDtypeStruct((t, FOX_HEADS * FOX_DH), BF16),
        scratch_shapes=[
            pltpu.VMEM((heads, tq, LANES), F32),
            pltpu.VMEM((heads, tq, LANES), F32),
            pltpu.VMEM((heads, tq, FOX_DH), F32),
        ],
        compiler_params=_params("parallel", "parallel", "arbitrary"),
        name="fox_attn",
    )(fqkv, fqkv, fqkv, cq, ck)


def _dsa_kernel(q_ref, k_ref, v_ref, iq_ref, ik_ref, iw_ref, o_ref, sc_ref, m_ref, l_ref, acc_ref,
                *, tq, tk, top_k, max_iters):
    qi = pl.program_id(1)
    n_kc = (qi * tq + tq + tk - 1) // tk
    group = DSA_Q_HEADS // DSA_KV_HEADS
    qpos = qi * tq + lax.broadcasted_iota(jnp.int32, (tq, 1), 0)

    iq = iq_ref[0].reshape(IDX_HEADS * tq, IDX_DH)
    iw = iw_ref[:, IDX_DH:IDX_DH + IDX_HEADS] * (IDX_HEADS ** -0.5 * IDX_DH ** -0.5)

    def score_chunk(c, carry):
        lo, hi = carry
        ks = pl.multiple_of(c * tk, tk)
        ik = ik_ref[pl.ds(ks, tk), 0:IDX_DH].astype(BF16)
        rel = _nt_dot(iq, ik)
        score = jnp.zeros((tq, tk), F32)
        for h in range(IDX_HEADS):
            score = score + iw[:, h:h + 1] * jnp.maximum(rel[h * tq:(h + 1) * tq, :], 0.0)
        kpos = ks + lax.broadcasted_iota(jnp.int32, (tq, tk), 1)
        valid = kpos <= qpos
        sc_ref[:, pl.ds(ks, tk)] = jnp.where(valid, score, NEG_INF)
        lo = jnp.minimum(lo, jnp.min(jnp.where(valid, score, jnp.inf), axis=-1, keepdims=True))
        hi = jnp.maximum(hi, jnp.max(jnp.where(valid, score, NEG_INF), axis=-1, keepdims=True))
        return lo, hi

    lo0, hi0 = lax.fori_loop(0, n_kc, score_chunk,
                             (jnp.full((tq, 1), jnp.inf, F32), jnp.full((tq, 1), NEG_INF, F32)))

    lane_reps = tk // LANES

    def count_ge(thr):
        thr_b = jnp.tile(jnp.broadcast_to(thr, (tq, LANES)), (1, lane_reps))

        def body(c, acc):
            ks = pl.multiple_of(c * tk, tk)
            hit = jnp.where(sc_ref[:, pl.ds(ks, tk)] >= thr_b, 1.0, 0.0)
            for r in range(lane_reps):
                acc = acc + hit[:, r * LANES:(r + 1) * LANES]
            return acc
        partial = lax.fori_loop(0, n_kc, body, jnp.zeros((tq, LANES), F32))
        return jnp.sum(partial, axis=-1, keepdims=True)

    k_f = float(top_k)
    n_keys = (qpos + 1).astype(F32)

    def unsettled(cnt_lo):
        return jnp.max(jnp.where(cnt_lo > k_f, 1.0, 0.0)) > 0.0

    def cond(state):
        it, _, _, cnt_lo = state
        return jnp.logical_and(it < max_iters, unsettled(cnt_lo))

    def refine(state):
        it, lo, hi, cnt_lo = state
        mid = 0.5 * (lo + hi)
        cnt = count_ge(mid)
        keep = cnt >= k_f
        lo = jnp.where(keep, mid, lo)
        cnt_lo = jnp.where(keep, cnt, cnt_lo)
        hi = jnp.where(keep, hi, mid)
        return it + 1, lo, hi, cnt_lo

    _, thr, _, _ = lax.while_loop(cond, refine, (jnp.int32(0), lo0, hi0, n_keys))

    m_ref[...] = jnp.full_like(m_ref, MASK_BIAS)
    l_ref[...] = jnp.zeros_like(l_ref)
    acc_ref[...] = jnp.zeros_like(acc_ref)
    c_exp = DSA_DH ** -0.5 * LOG2_E
    thr_b = jnp.tile(jnp.broadcast_to(thr, (tq, LANES)), (1, lane_reps))

    def attend(c, carry):
        ks = pl.multiple_of(c * tk, tk)
        bias = jnp.where(sc_ref[:, pl.ds(ks, tk)] >= thr_b, 0.0, MASK_BIAS)
        for g in range(DSA_KV_HEADS):
            heads = slice(g * group, (g + 1) * group)
            qg = jnp.concatenate(
                [q_ref[:, (g * group + j) * DSA_DH:(g * group + j + 1) * DSA_DH] for j in range(group)], axis=0)
            kc = k_ref[pl.ds(ks, tk), g * DSA_DH:(g + 1) * DSA_DH]
            vc = v_ref[pl.ds(ks, tk), g * DSA_DH:(g + 1) * DSA_DH]
            s = _nt_dot(qg, kc).reshape(group, tq, tk) + bias[None]
            m_prev = m_ref[heads]
            m_new = jnp.maximum(m_prev, jnp.max(s, axis=-1, keepdims=True))
            p = jnp.exp2((s - jnp.tile(m_new, (1, 1, lane_reps))) * c_exp)
            alpha = jnp.exp2((m_prev - m_new) * c_exp)
            p_sum = p[:, :, 0:LANES]
            for r in range(1, lane_reps):
                p_sum = p_sum + p[:, :, r * LANES:(r + 1) * LANES]
            l_ref[heads] = alpha * l_ref[heads] + p_sum
            pv = jnp.dot(p.reshape(group * tq, tk).astype(BF16), vc, preferred_element_type=F32)
            acc_ref[heads] = alpha * acc_ref[heads] + pv.reshape(group, tq, DSA_DH)
            m_ref[heads] = m_new
        return carry

    lax.fori_loop(0, n_kc, attend, 0)
    for h in range(DSA_Q_HEADS):
        l = jnp.sum(l_ref[h], axis=-1, keepdims=True)
        o_ref[:, h * DSA_DH:(h + 1) * DSA_DH] = (acc_ref[h] / l).astype(o_ref.dtype)


def _dsa(qk, v, iq_hm, small, batch, seq_len, tq=128, tk=512):
    t = qk.shape[0]
    tk = min(tk, seq_len)
    nq = seq_len // tq
    qw = DSA_Q_HEADS * DSA_DH
    kw = DSA_KV_HEADS * DSA_DH
    top_k = min(DSA_TOPK_MAX, seq_len // 4)
    return pl.pallas_call(
        functools.partial(_dsa_kernel, tq=tq, tk=tk, top_k=top_k, max_iters=64),
        grid=(batch, nq),
        in_specs=[
            pl.BlockSpec((tq, qw), lambda b, i: (b * nq + i, 0)),
            pl.BlockSpec((seq_len, kw), lambda b, i: (b, qw // kw)),
            pl.BlockSpec((seq_len, kw), lambda b, i: (b, 0)),
            pl.BlockSpec((1, IDX_HEADS, tq, IDX_DH), lambda b, i: (b, 0, i, 0)),
            pl.BlockSpec((seq_len, LANES), lambda b, i: (b, 0)),
            pl.BlockSpec((tq, LANES), lambda b, i: (b * nq + i, 0)),
        ],
        out_specs=pl.BlockSpec((tq, qw), lambda b, i: (b * nq + i, 0)),
        out_shape=jax.ShapeDtypeStruct((t, qw), BF16),
        scratch_shapes=[
            pltpu.VMEM((tq, seq_len), F32),
            pltpu.VMEM((DSA_Q_HEADS, tq, LANES), F32),
            pltpu.VMEM((DSA_Q_HEADS, tq, LANES), F32),
            pltpu.VMEM((DSA_Q_HEADS, tq, DSA_DH), F32),
        ],
        compiler_params=_params("parallel", "arbitrary"),
        name="dsa",
    )(qk, qk, v, iq_hm, small, small)


def _router_kernel(x_ref, w_ref, b_ref, o_ref):
    logits = jnp.dot(x_ref[...], w_ref[...], preferred_element_type=F32,
                     precision=lax.Precision.HIGHEST) + b_ref[...]
    lane = lax.broadcasted_iota(jnp.int32, logits.shape, 1)
    big = jnp.int32(LANES)
    g_logits = jnp.where(lane < N_GROUPS, logits, NEG_INF)
    g_max = jnp.max(g_logits, axis=-1, keepdims=True)
    g_prob = 1.0 / jnp.sum(jnp.exp(g_logits - g_max), axis=-1, keepdims=True)
    g_idx = jnp.min(jnp.where(g_logits == g_max, lane, big), axis=-1, keepdims=True)
    first = N_GROUPS + g_idx * EXPERTS_PER_GROUP
    e_logits = jnp.where((lane >= first) & (lane < first + EXPERTS_PER_GROUP), logits, NEG_INF)
    v1 = jnp.max(e_logits, axis=-1, keepdims=True)
    i1 = jnp.min(jnp.where(e_logits == v1, lane, big), axis=-1, keepdims=True)
    rest = jnp.where(lane == i1, NEG_INF, e_logits)
    v2 = jnp.max(rest, axis=-1, keepdims=True)
    i2 = jnp.min(jnp.where(rest == v2, lane, big), axis=-1, keepdims=True)
    e2 = jnp.exp(v2 - v1)
    w1 = (1.0 / (1.0 + e2)) * g_prob
    w2 = (e2 / (1.0 + e2)) * g_prob
    out = jnp.where(lane == 0, (i1 - N_GROUPS).astype(F32), 0.0)
    out = jnp.where(lane == 1, (i2 - N_GROUPS).astype(F32), out)
    out = jnp.where(lane == 2, w1, out)
    out = jnp.where(lane == 3, w2, out)
    o_ref[...] = out


def _router(x, w_cat, b_cat, tm=512):
    t, d = x.shape
    return pl.pallas_call(
        _router_kernel,
        grid=(t // tm,),
        in_specs=[pl.BlockSpec((tm, d), lambda i: (i, 0)),
                  pl.BlockSpec((d, LANES), lambda i: (0, 0)),
                  pl.BlockSpec((1, LANES), lambda i: (0, 0))],
        out_specs=pl.BlockSpec((tm, LANES), lambda i: (i, 0)),
        out_shape=jax.ShapeDtypeStruct((t, LANES), F32),
        compiler_params=_params("parallel"),
        name="moe_router",
    )(x, w_cat, b_cat)


def _plan_kernel(pos_ref, src_ref, dst_ref, *, n_tokens):
    def body(a, c):
        p = pos_ref[a]
        src_ref[p] = jnp.where(a >= n_tokens, a - n_tokens, a)
        dst_ref[p] = a
        return c

    lax.fori_loop(0, 2 * n_tokens, body, 0, unroll=8)


def _plan(pos, n_tokens):
    smem = pl.BlockSpec(memory_space=pltpu.SMEM)
    return pl.pallas_call(
        functools.partial(_plan_kernel, n_tokens=n_tokens),
        in_specs=[smem],
        out_specs=[smem, smem],
        out_shape=[jax.ShapeDtypeStruct((2 * n_tokens,), jnp.int32)] * 2,
        name="moe_plan",
    )(pos)


def _for_rows(n, tm, fn):
    def body(i, c):
        fn(i)
        return c

    @pl.when(n == tm)
    def _():
        lax.fori_loop(0, tm, body, 0, unroll=8)

    @pl.when(n < tm)
    def _():
        lax.fori_loop(0, n, body, 0)


def _experts_kernel(tile_e_ref, tile_start_ref, tile_n_ref, src_ref, dst_ref,
                    x_hbm, wg_ref, wu_ref, wd_ref, y_hbm, xbuf, ybuf, gsem, ssem, *, tm, n_tiles):
    del tile_e_ref
    t = pl.program_id(0)
    slot = t % 2

    def gather_row(tile, s, i):
        row = src_ref[tile_start_ref[tile] + i]
        return pltpu.make_async_copy(x_hbm.at[pl.ds(row, 1)], xbuf.at[s, pl.ds(i, 1)], gsem.at[s])

    def scatter_row(tile, s, i):
        row = dst_ref[tile_start_ref[tile] + i]
        return pltpu.make_async_copy(ybuf.at[s, pl.ds(i, 1)], y_hbm.at[pl.ds(row, 1)], ssem.at[s])

    def start_gather(tile, s):
        _for_rows(tile_n_ref[tile], tm, lambda i: gather_row(tile, s, i).start())

    def wait_rows(tile, bulk_copy, row_copy):
        n = tile_n_ref[tile]
        n8 = pl.multiple_of(n // SUBLANES * SUBLANES, SUBLANES)

        @pl.when(n8 > 0)
        def _():
            bulk_copy(n8).wait()

        def tail(i, c):
            row_copy(i).wait()
            return c
        lax.fori_loop(n8, n, tail, 0)

    def wait_gather(tile, s):
        wait_rows(tile,
                  lambda n8: pltpu.make_async_copy(x_hbm.at[pl.ds(0, n8)], xbuf.at[s, pl.ds(0, n8)], gsem.at[s]),
                  lambda i: gather_row(tile, s, i))

    def wait_scatter(tile, s):
        wait_rows(tile,
                  lambda n8: pltpu.make_async_copy(ybuf.at[s, pl.ds(0, n8)], y_hbm.at[pl.ds(0, n8)], ssem.at[s]),
                  lambda i: scatter_row(tile, s, i))

    @pl.when(t == 0)
    def _():
        xbuf[...] = jnp.zeros_like(xbuf)
        start_gather(0, 0)

    @pl.when(t + 1 < n_tiles)
    def _():
        start_gather(t + 1, 1 - slot)

    @pl.when(t >= 2)
    def _():
        wait_scatter(t - 2, slot)

    n_rows = tile_n_ref[t]

    @pl.when(n_rows > 0)
    def _():
        wait_gather(t, slot)
        xb = xbuf[slot].astype(BF16)
        gate = jnp.dot(xb, wg_ref[0, 0, 0].astype(BF16), preferred_element_type=F32)
        up = jnp.dot(xb, wu_ref[0, 0, 0].astype(BF16), preferred_element_type=F32)
        hidden = (_silu(gate) * up).astype(BF16)
        ybuf[slot] = jnp.dot(hidden, wd_ref[0, 0, 0].astype(BF16), preferred_element_type=F32)
        _for_rows(n_rows, tm, lambda i: scatter_row(t, slot, i).start())

    @pl.when(t == n_tiles - 1)
    def _():
        if n_tiles >= 2:
            wait_scatter(t - 1, 1 - slot)
        wait_scatter(t, slot)


def _experts(x, w_gate, w_up, w_down, layer, tile_e, tile_start, tile_n, src_rows, dst_rows, tm, n_tiles):
    t_tokens, d = x.shape
    ff = w_gate.shape[-1]
    w_map = lambda t, te, ts, tn, sr, ds_: (layer, te[t] // EXPERTS_PER_GROUP, te[t] % EXPERTS_PER_GROUP, 0, 0)
    grid_spec = pltpu.PrefetchScalarGridSpec(
        num_scalar_prefetch=5,
        grid=(n_tiles,),
        in_specs=[
            pl.BlockSpec(memory_space=pl.ANY),
            pl.BlockSpec((1, 1, 1, d, ff), w_map),
            pl.BlockSpec((1, 1, 1, d, ff), w_map),
            pl.BlockSpec((1, 1, 1, ff, d), w_map),
        ],
        out_specs=pl.BlockSpec(memory_space=pl.ANY),
        scratch_shapes=[
            pltpu.VMEM((2, tm, d), F32),
            pltpu.VMEM((2, tm, d), F32),
            pltpu.SemaphoreType.DMA((2,)),
            pltpu.SemaphoreType.DMA((2,)),
        ],
    )
    return pl.pallas_call(
        functools.partial(_experts_kernel, tm=tm, n_tiles=n_tiles),
        grid_spec=grid_spec,
        out_shape=jax.ShapeDtypeStruct((2 * t_tokens, d), F32),
        compiler_params=_params("arbitrary"),
        name="moe_experts",
    )(tile_e, tile_start, tile_n, src_rows, dst_rows, x, w_gate, w_up, w_down)


def _combine_ln_kernel(x_ref, y0_ref, y1_ref, r_ref, g_ref, b_ref, o_ref):
    w0 = r_ref[:, 2:3]
    w1 = r_ref[:, 3:4]
    y = DN_ALPHA * x_ref[...] + (w0 * y0_ref[...] + w1 * y1_ref[...])
    o_ref[...] = _layer_norm(y, g_ref[...], b_ref[...])


def _combine_ln(x, y2, route, g, b, tm=512):
    t, d = x.shape
    n_t = t // tm
    return pl.pallas_call(
        _combine_ln_kernel,
        grid=(n_t,),
        in_specs=[
            pl.BlockSpec((tm, d), lambda i: (i, 0)),
            pl.BlockSpec((tm, d), lambda i: (i, 0)),
            pl.BlockSpec((tm, d), lambda i: (i + n_t, 0)),
            pl.BlockSpec((tm, LANES), lambda i: (i, 0)),
            pl.BlockSpec((1, d), lambda i: (0, 0)),
            pl.BlockSpec((1, d), lambda i: (0, 0)),
        ],
        out_specs=pl.BlockSpec((tm, d), lambda i: (i, 0)),
        out_shape=jax.ShapeDtypeStruct((t, d), F32),
        compiler_params=_params("parallel"),
        name="moe_combine_ln",
    )(x, y2, y2, route, g.reshape(1, d), b.reshape(1, d))


def _moe_layer(x, wg, bg, we, be, w_gate, w_up, w_down, layer, ln_g, ln_b, tm=256):
    t, d = x.shape
    w_cat = jnp.concatenate([wg, jnp.transpose(we, (1, 0, 2)).reshape(d, N_EXPERTS)], axis=1)
    w_cat = jnp.pad(w_cat, ((0, 0), (0, LANES - w_cat.shape[1])))
    b_cat = jnp.pad(jnp.concatenate([bg, be.reshape(-1)]), (0, LANES - N_GROUPS - N_EXPERTS)).reshape(1, LANES)
    route = _router(x, w_cat, b_cat)

    i32 = jnp.int32
    expert = jnp.concatenate([route[:, 0], route[:, 1]]).astype(i32)
    e_ids = jnp.arange(N_EXPERTS, dtype=i32)
    onehot = (expert[:, None] == e_ids[None, :]).astype(i32)
    running = jnp.cumsum(onehot, axis=0)
    counts = running[-1]
    seg_start = jnp.cumsum(counts) - counts
    pos = jnp.sum((running - onehot + seg_start[None, :]) * onehot, axis=1)
    tiles_per = (counts + tm - 1) // tm
    tile_end = jnp.cumsum(tiles_per)
    tile_begin = tile_end - tiles_per
    n_tiles = (2 * t) // tm + N_EXPERTS
    tile_ids = jnp.arange(n_tiles, dtype=i32)
    n_used = tile_end[-1]
    tile_e = jnp.sum((tile_end[None, :] <= jnp.minimum(tile_ids, n_used - 1)[:, None]).astype(i32), axis=1)
    tile_e = jnp.clip(tile_e, 0, N_EXPERTS - 1)
    te_onehot = (tile_e[:, None] == e_ids[None, :]).astype(i32)
    within = (tile_ids - jnp.sum(te_onehot * tile_begin[None, :], axis=1)) * tm
    tile_n = jnp.clip(jnp.sum(te_onehot * counts[None, :], axis=1) - within, 0, tm)
    tile_n = jnp.where(tile_ids < n_used, tile_n, 0).astype(i32)
    tile_start = jnp.where(tile_n > 0, jnp.sum(te_onehot * seg_start[None, :], axis=1) + within, 0).astype(i32)
    src_rows, dst_rows = _plan(pos.astype(i32), t)
    y2 = _experts(x, w_gate, w_up, w_down, layer, tile_e.astype(i32), tile_start, tile_n, src_rows, dst_rows,
                  tm, n_tiles)
    return _combine_ln(x, y2, route, ln_g, ln_b)


def _even_mixer_layer(x, batch, seq_len, w_in, gate_w2, gate_b, norm_g, fox_gate_b, w_out, ln_g, ln_b):
    d = x.shape[1]
    gla_w = 2 * GLA_HEADS * GLA_DK + 2 * GLA_HEADS * GLA_DV
    fox_w = 3 * FOX_HEADS * FOX_DH
    glr0 = gla_w
    fox0 = glr0 + GLA_GATE_RANK
    ff0 = fox0 + fox_w
    main = _project(x, w_in, 0, gla_w, F32, seq_len)
    fqkv = _project(x, w_in[:, fox0:ff0], 0, fox_w, BF16, seq_len)
    w_small = jnp.concatenate([w_in[:, ff0:ff0 + FOX_HEADS], w_in[:, glr0:fox0]], axis=1)
    w_small = jnp.pad(w_small, ((0, 0), (0, LANES - w_small.shape[1])))
    small = _project(x, w_small, 0, LANES, F32, seq_len)
    o_gla = _gla(main, small, gate_w2, gate_b, norm_g, batch, seq_len)
    gate_b_row = jnp.pad(fox_gate_b, (0, LANES - FOX_HEADS)).reshape(1, LANES)
    c, c_t = _fox_gate(small, gate_b_row, batch, seq_len)
    fox_heads_per_step = 4
    cq = c.reshape(batch, seq_len, LANES)[:, :, :FOX_HEADS]
    cq = jnp.transpose(cq.reshape(batch, seq_len, FOX_HEADS // fox_heads_per_step, fox_heads_per_step), (0, 2, 1, 3))
    ck = c_t[:, :FOX_HEADS, :].reshape(batch * FOX_HEADS, 1, seq_len)
    o_fox = _fox(fqkv, cq, ck, batch, seq_len, heads=fox_heads_per_step)
    return _out_proj_ln(o_gla, o_fox, 0, w_out, x, ln_g, ln_b)


def _odd_mixer_layer(x, batch, seq_len, w_in, w_out, ln_g, ln_b):
    qw = DSA_Q_HEADS * DSA_DH
    kw = DSA_KV_HEADS * DSA_DH
    iqw = IDX_HEADS * IDX_DH
    rope_attn = (DSA_DH,) + _rope_tables(seq_len, DSA_DH)
    rope_idx = (IDX_DH,) + _rope_tables(seq_len, IDX_DH)
    qk = _project(x, w_in, 0, qw + kw, BF16, seq_len, rope=rope_attn)
    v = _project(x, w_in, qw + kw, kw, BF16, seq_len)
    iq = _project(x, w_in, qw + 2 * kw, iqw, BF16, seq_len, rope=rope_idx)
    small0 = qw + 2 * kw + iqw
    w_small = jnp.pad(w_in[:, small0:], ((0, 0), (0, LANES - (w_in.shape[1] - small0))))
    small = _project(x, w_small, 0, LANES, F32, seq_len, rope=rope_idx, rope_lanes=IDX_DH)
    iq_hm = jnp.transpose(iq.reshape(batch, seq_len, IDX_HEADS, IDX_DH), (0, 2, 1, 3))
    o = _dsa(qk, v, iq_hm, small, batch, seq_len)
    return _out_proj_ln(o, o, qw // 2, w_out, x, ln_g, ln_b)


def kernel(x, a_w_in, a_gla_gate_w2, a_gla_gate_b, a_gla_norm_g, a_fox_gate_b, a_w_out, c_w_in, c_w_out,
           ln_mix_g, ln_mix_b, ln_ffn_g, ln_ffn_b, moe_group_w, moe_group_b, moe_expert_w, moe_expert_b,
           moe_w_gate, moe_w_up, moe_w_down):
    batch, seq_len, d = x.shape
    h = x.reshape(batch * seq_len, d)
    for layer in range(DEPTH):
        j = layer // 2
        if layer % 2 == 0:
            h = _even_mixer_layer(h, batch, seq_len, a_w_in[j], a_gla_gate_w2[j], a_gla_gate_b[j],
                                  a_gla_norm_g[j], a_fox_gate_b[j], a_w_out[j], ln_mix_g[layer], ln_mix_b[layer])
        else:
            h = _odd_mixer_layer(h, batch, seq_len, c_w_in[j], c_w_out[j], ln_mix_g[layer], ln_mix_b[layer])
        h = _moe_layer(h, moe_group_w[layer], moe_group_b[layer], moe_expert_w[layer], moe_expert_b[layer],
                       moe_w_gate, moe_w_up, moe_w_down, layer, ln_ffn_g[layer], ln_ffn_b[layer])
    return h.reshape(batch, seq_len, d)
```

```python
import functools

import jax
import jax.numpy as jnp
import numpy as np
from jax import lax
from jax.experimental import pallas as pl
from jax.experimental.pallas import tpu as pltpu

F32 = jnp.float32
BF16 = jnp.bfloat16

DEPTH = 2
GLA_HEADS = 4
GLA_DK = 128
GLA_DV = 256
GLA_GATE_RANK = 16
GLA_GATE_TAU = 16.0
GLA_CHUNK = 64
FOX_HEADS = 8
FOX_DH = 128
DSA_Q_HEADS = 16
DSA_KV_HEADS = 4
DSA_DH = 128
IDX_HEADS = 16
IDX_DH = 64
DSA_TOPK_MAX = 256
N_GROUPS = 4
EXPERTS_PER_GROUP = 8
N_EXPERTS = N_GROUPS * EXPERTS_PER_GROUP
ROPE_THETA = 10000.0
LN_EPS = 1e-5
DN_ALPHA = (2 * DEPTH) ** 0.25

LANES = 128
SUBLANES = 8
VMEM_LIMIT_BYTES = 56 * 1024 * 1024

NEG_INF = float("-inf")
LOG2_E = 1.4426950408889634
MASK_BIAS = -1e30


def _params(*semantics):
    return pltpu.CompilerParams(dimension_semantics=semantics, vmem_limit_bytes=VMEM_LIMIT_BYTES)


def _nt_dot(a, b):
    return lax.dot_general(a, b, (((1,), (1,)), ((), ())), preferred_element_type=F32)


def _tn_dot(a, b):
    return lax.dot_general(a, b, (((0,), (0,)), ((), ())), preferred_element_type=F32)


def _log_sigmoid(z):
    return jnp.minimum(z, 0.0) - jnp.log1p(jnp.exp(-jnp.abs(z)))


def _silu(z):
    return z * (1.0 / (1.0 + jnp.exp(-z)))


def _cumsum_rows(x):
    n = x.shape[0]
    row = lax.broadcasted_iota(jnp.int32, x.shape, 0)
    shift = 1
    while shift < n:
        x = x + jnp.where(row >= shift, pltpu.roll(x, shift, 0), 0.0)
        shift *= 2
    return x


def _rot_half(x, head_dim):
    half = head_dim // 2
    n = x.shape[-1]
    lane = lax.broadcasted_iota(jnp.int32, x.shape, 1)
    fwd = pltpu.roll(x, n - half, 1)
    bwd = pltpu.roll(x, half, 1)
    return jnp.where((lane % head_dim) < half, fwd, bwd)


def _layer_norm(y, g, b):
    mu = jnp.mean(y, axis=-1, keepdims=True)
    d = y - mu
    var = jnp.mean(d * d, axis=-1, keepdims=True)
    return d * lax.rsqrt(var + LN_EPS) * g + b


def _cast_kernel(x_ref, o_ref):
    o_ref[...] = x_ref[...].astype(o_ref.dtype)


def _to_bf16(a, rows=512):
    r, c = a.shape
    rows = min(rows, r)
    return pl.pallas_call(
        _cast_kernel,
        grid=(r // rows,),
        in_specs=[pl.BlockSpec((rows, c), lambda i: (i, 0))],
        out_specs=pl.BlockSpec((rows, c), lambda i: (i, 0)),
        out_shape=jax.ShapeDtypeStruct((r, c), BF16),
        compiler_params=_params("parallel"),
        name="to_bf16",
    )(a)


def _proj_kernel(*refs, rope_dim, rope_lanes):
    if rope_dim:
        x_ref, w_ref, cos_ref, sin_ref, o_ref = refs
    else:
        x_ref, w_ref, o_ref = refs
    acc = jnp.dot(x_ref[...], w_ref[...].astype(BF16), preferred_element_type=F32)
    if rope_dim:
        reps = acc.shape[-1] // LANES
        cos = jnp.tile(cos_ref[...], (1, reps))
        sin = jnp.tile(sin_ref[...], (1, reps))
        roped = acc * cos + _rot_half(acc, rope_dim) * sin
        if rope_lanes < acc.shape[-1]:
            lane = lax.broadcasted_iota(jnp.int32, acc.shape, 1)
            roped = jnp.where(lane < rope_lanes, roped, acc)
        acc = roped
    o_ref[...] = acc.astype(o_ref.dtype)


def _project(x, w, col_start, n_cols, out_dtype, seq_len, rope=None, rope_lanes=None, tm=1024, tn=512):
    t, k = x.shape
    tm = min(tm, seq_len)
    tn = min(tn, n_cols)
    assert t % tm == 0 and n_cols % tn == 0 and col_start % tn == 0 and seq_len % tm == 0
    col_off = col_start // tn
    in_specs = [
        pl.BlockSpec((tm, k), lambda i, j: (i, 0)),
        pl.BlockSpec((k, tn), lambda i, j: (0, j + col_off)),
    ]
    args = [x, w]
    rope_dim = 0
    if rope is not None:
        rope_dim, cos, sin = rope
        seq_tiles = seq_len // tm
        in_specs += [pl.BlockSpec((tm, LANES), lambda i, j: (i % seq_tiles, 0))] * 2
        args += [cos, sin]
    return pl.pallas_call(
        functools.partial(_proj_kernel, rope_dim=rope_dim, rope_lanes=rope_lanes or n_cols),
        grid=(t // tm, n_cols // tn),
        in_specs=in_specs,
        out_specs=pl.BlockSpec((tm, tn), lambda i, j: (i, j)),
        out_shape=jax.ShapeDtypeStruct((t, n_cols), out_dtype),
        compiler_params=_params("parallel", "parallel"),
        name="in_proj",
    )(*args)


def _rope_tables(seq_len, head_dim):
    half = head_dim // 2
    inv = ROPE_THETA ** (-jnp.arange(half, dtype=F32) / half)
    ang = jnp.arange(seq_len, dtype=F32)[:, None] * inv[None, :]
    cos, sin = jnp.cos(ang), jnp.sin(ang)
    reps = LANES // head_dim
    cos_full = jnp.tile(jnp.concatenate([cos, cos], axis=-1), (1, reps))
    sin_signed = jnp.tile(jnp.concatenate([-sin, sin], axis=-1), (1, reps))
    return cos_full, sin_signed


def _out_proj_ln_kernel(oa_ref, ob_ref, w_ref, x_ref, g_ref, b_ref, out_ref, out_bf_ref):
    half = oa_ref.shape[1]
    h = jnp.dot(oa_ref[...], w_ref[0:half, :], preferred_element_type=F32)
    h = h + jnp.dot(ob_ref[...], w_ref[half:2 * half, :], preferred_element_type=F32)
    out = _layer_norm(DN_ALPHA * x_ref[...] + h, g_ref[...], b_ref[...])
    out_ref[...] = out
    out_bf_ref[...] = out.astype(BF16)


def _out_proj_ln(o_a, o_b, b_col_block, w_bf16, x, g, b, tm=512):
    t = o_a.shape[0]
    kdim, d = w_bf16.shape
    half = kdim // 2
    tm = min(tm, t)
    return pl.pallas_call(
        _out_proj_ln_kernel,
        grid=(t // tm,),
        in_specs=[
            pl.BlockSpec((tm, half), lambda i: (i, 0)),
            pl.BlockSpec((tm, half), lambda i: (i, b_col_block)),
            pl.BlockSpec((kdim, d), lambda i: (0, 0)),
            pl.BlockSpec((tm, d), lambda i: (i, 0)),
            pl.BlockSpec((1, d), lambda i: (0, 0)),
            pl.BlockSpec((1, d), lambda i: (0, 0)),
        ],
        out_specs=[pl.BlockSpec((tm, d), lambda i: (i, 0)), pl.BlockSpec((tm, d), lambda i: (i, 0))],
        out_shape=[jax.ShapeDtypeStruct((t, d), F32), jax.ShapeDtypeStruct((t, d), BF16)],
        compiler_params=_params("parallel"),
        name="out_proj_ln",
    )(o_a, o_b, w_bf16, x, g.reshape(1, d), b.reshape(1, d))


def _gla_kernel(q_ref, k_ref, v_ref, r_ref, sm_ref, w2_ref, gb_ref, ng_ref, o_ref, state_ref, *, n_chunks):
    c_len, dk, dv = GLA_CHUNK, GLA_DK, GLA_DV

    @pl.when(pl.program_id(1) == 0)
    def _():
        state_ref[...] = jnp.zeros_like(state_ref)

    row = lax.broadcasted_iota(jnp.int32, (c_len, c_len), 0)
    col = lax.broadcasted_iota(jnp.int32, (c_len, c_len), 1)
    causal = row >= col
    norm_g = ng_ref[...]
    for c in range(n_chunks):
        rows = slice(c * c_len, (c + 1) * c_len)
        glr = sm_ref[rows, FOX_HEADS:FOX_HEADS + GLA_GATE_RANK].astype(BF16)
        for h in range(GLA_HEADS):
            kcols = slice(h * dk, (h + 1) * dk)
            vcols = slice(h * dv, (h + 1) * dv)
            z = jnp.dot(glr, w2_ref[:, kcols].astype(BF16), preferred_element_type=F32) + gb_ref[:, kcols]
            log_a = _log_sigmoid(z) / GLA_GATE_TAU
            b = _cumsum_rows(log_a)
            b_last = b[c_len - 1:c_len, :]
            q = q_ref[rows, kcols] * (dk ** -0.5)
            k = k_ref[rows, kcols]
            v = v_ref[rows, vcols].astype(BF16)
            q_dec = (q * jnp.exp(b)).astype(BF16)
            k_inv = (k * jnp.exp(-b)).astype(BF16)
            k_end = (k * jnp.exp(b_last - b)).astype(BF16)
            scores = jnp.where(causal, _nt_dot(q_dec, k_inv), 0.0)
            state_t = state_ref[h]
            o = jnp.dot(scores.astype(BF16), v, preferred_element_type=F32)
            o = o + _nt_dot(q_dec, state_t.astype(BF16))
            state_ref[h] = jnp.exp(b_last) * state_t + _tn_dot(v, k_end)
            rms = lax.rsqrt(jnp.mean(o * o, axis=-1, keepdims=True) + LN_EPS)
            o_ref[rows, vcols] = (o * rms * norm_g * _silu(r_ref[rows, vcols])).astype(o_ref.dtype)


def _gla(main, small, gate_w2, gate_b, norm_g, batch, seq_len, rows_per_step=256):
    t = main.shape[0]
    n_steps = seq_len // rows_per_step
    hk, hv = GLA_HEADS * GLA_DK, GLA_HEADS * GLA_DV
    row_map = lambda b, n: b * n_steps + n
    return pl.pallas_call(
        functools.partial(_gla_kernel, n_chunks=rows_per_step // GLA_CHUNK),
        grid=(batch, n_steps),
        in_specs=[
            pl.BlockSpec((rows_per_step, hk), lambda b, n: (row_map(b, n), 0)),
            pl.BlockSpec((rows_per_step, hk), lambda b, n: (row_map(b, n), 1)),
            pl.BlockSpec((rows_per_step, hv), lambda b, n: (row_map(b, n), 1)),
            pl.BlockSpec((rows_per_step, hv), lambda b, n: (row_map(b, n), 2)),
            pl.BlockSpec((rows_per_step, LANES), lambda b, n: (row_map(b, n), 0)),
            pl.BlockSpec((GLA_GATE_RANK, hk), lambda b, n: (0, 0)),
            pl.BlockSpec((1, hk), lambda b, n: (0, 0)),
            pl.BlockSpec((1, GLA_DV), lambda b, n: (0, 0)),
        ],
        out_specs=pl.BlockSpec((rows_per_step, hv), lambda b, n: (row_map(b, n), 0)),
        out_shape=jax.ShapeDtypeStruct((t, hv), BF16),
        scratch_shapes=[pltpu.VMEM((GLA_HEADS, GLA_DV, GLA_DK), F32)],
        compiler_params=_params("parallel", "arbitrary"),
        name="gla",
    )(main, main, main, main, small, gate_w2, gate_b.reshape(1, hk), norm_g.reshape(1, GLA_DV))


def _fox_gate_kernel(sm_ref, b_ref, c_ref, ct_ref):
    c = _cumsum_rows(_log_sigmoid(sm_ref[...] + b_ref[...]))
    c_ref[...] = c
    ct_ref[0] = c.T


def _fox_gate(small, gate_b_row, batch, seq_len):
    t = small.shape[0]
    return pl.pallas_call(
        _fox_gate_kernel,
        grid=(batch,),
        in_specs=[pl.BlockSpec((seq_len, LANES), lambda b: (b, 0)), pl.BlockSpec((1, LANES), lambda b: (0, 0))],
        out_specs=[pl.BlockSpec((seq_len, LANES), lambda b: (b, 0)),
                   pl.BlockSpec((1, LANES, seq_len), lambda b: (b, 0, 0))],
        out_shape=[jax.ShapeDtypeStruct((t, LANES), F32), jax.ShapeDtypeStruct((batch, LANES, seq_len), F32)],
        compiler_params=_params("parallel"),
        name="fox_gate",
    )(small, gate_b_row)


def _fox_kernel(q_ref, k_ref, v_ref, cq_ref, ck_ref, o_ref, m_ref, l_ref, acc_ref, *, tq, heads):
    qi = pl.program_id(2)
    dh = FOX_DH
    scale = dh ** -0.5
    inv_scale = dh ** 0.5
    c_exp = scale * LOG2_E
    reps = tq // LANES

    m_ref[...] = jnp.full_like(m_ref, MASK_BIAS)
    l_ref[...] = jnp.zeros_like(l_ref)
    acc_ref[...] = jnp.zeros_like(acc_ref)

    def step(j, masked):
        ks = pl.multiple_of(j * tq, tq)
        for h in range(heads):
            cols = slice(h * dh, (h + 1) * dh)
            q = q_ref[:, cols]
            k = k_ref[pl.ds(ks, tq), cols]
            v = v_ref[pl.ds(ks, tq), cols]
            cq = cq_ref[0, 0, :, h:h + 1] * inv_scale
            s = _nt_dot(q, k) - ck_ref[h, :, pl.ds(ks, tq)] * inv_scale
            if masked:
                row = lax.broadcasted_iota(jnp.int32, s.shape, 0)
                col = lax.broadcasted_iota(jnp.int32, s.shape, 1)
                s = jnp.where(col <= row, s, MASK_BIAS)
            m_prev = m_ref[h]
            m_new = jnp.maximum(m_prev, jnp.max(s, axis=-1, keepdims=True) + cq)
            shift = m_new - cq
            p = jnp.exp2((s - jnp.tile(shift, (1, reps))) * c_exp)
            alpha = jnp.exp2((m_prev - m_new) * c_exp)
            p_sum = p[:, 0:LANES]
            for r in range(1, reps):
                p_sum = p_sum + p[:, r * LANES:(r + 1) * LANES]
            l_ref[h] = alpha * l_ref[h] + p_sum
            acc_ref[h] = alpha * acc_ref[h] + jnp.dot(p.astype(BF16), v, preferred_element_type=F32)
            m_ref[h] = m_new

    def body(j, carry):
        step(j, False)
        return carry

    lax.fori_loop(0, qi, body, 0)
    step(qi, True)
    for h in range(heads):
        l = jnp.sum(l_ref[h], axis=-1, keepdims=True)
        o_ref[:, h * dh:(h + 1) * dh] = (acc_ref[h] / l).astype(o_ref.dtype)


def _fox(fqkv, cq, ck, batch, seq_len, tq=512, heads=4):
    t = fqkv.shape[0]
    nq = seq_len // tq
    n_hg = FOX_HEADS // heads
    w = heads * FOX_DH
    return pl.pallas_call(
        functools.partial(_fox_kernel, tq=tq, heads=heads),
        grid=(batch, n_hg, nq),
        in_specs=[
            pl.BlockSpec((tq, w), lambda b, g, i: (b * nq + i, g)),
            pl.BlockSpec((seq_len, w), lambda b, g, i: (b, n_hg + g)),
            pl.BlockSpec((seq_len, w), lambda b, g, i: (b, 2 * n_hg + g)),
            pl.BlockSpec((1, 1, tq, heads), lambda b, g, i: (b, g, i, 0)),
            pl.BlockSpec((heads, 1, seq_len), lambda b, g, i: (b * n_hg + g, 0, 0)),
        ],
        out_specs=pl.BlockSpec((tq, w), lambda b, g, i: (b * nq + i, g)),
        out_shape=jax.ShapeDtypeStruct((t, FOX_HEADS * FOX_DH), BF16),
        scratch_shapes=[
            pltpu.VMEM((heads, tq, LANES), F32),
            pltpu.VMEM((heads, tq, LANES), F32),
            pltpu.VMEM((heads, tq, FOX_DH), F32),
        ],
        compiler_params=_params("parallel", "parallel", "arbitrary"),
        name="fox_attn",
    )(fqkv, fqkv, fqkv, cq, ck)


def _dsa_kernel(q_ref, k_ref, v_ref, iq_ref, ik_ref, iw_ref, o_ref, sc_ref, m_ref, l_ref, acc_ref,
                *, tq, tk, top_k, max_iters):
    qi = pl.program_id(1)
    n_kc = (qi * tq + tq + tk - 1) // tk
    group = DSA_Q_HEADS // DSA_KV_HEADS
    qpos = qi * tq + lax.broadcasted_iota(jnp.int32, (tq, 1), 0)

    iq = iq_ref[0].reshape(IDX_HEADS * tq, IDX_DH)
    iw = iw_ref[:, IDX_DH:IDX_DH + IDX_HEADS] * (IDX_HEADS ** -0.5 * IDX_DH ** -0.5)

    def score_chunk(c, carry):
        lo, hi = carry
        ks = pl.multiple_of(c * tk, tk)
        ik = ik_ref[pl.ds(ks, tk), 0:IDX_DH].astype(BF16)
        rel = _nt_dot(iq, ik)
        score = jnp.zeros((tq, tk), F32)
        for h in range(IDX_HEADS):
            score = score + iw[:, h:h + 1] * jnp.maximum(rel[h * tq:(h + 1) * tq, :], 0.0)
        kpos = ks + lax.broadcasted_iota(jnp.int32, (tq, tk), 1)
        valid = kpos <= qpos
        sc_ref[:, pl.ds(ks, tk)] = jnp.where(valid, score, NEG_INF)
        lo = jnp.minimum(lo, jnp.min(jnp.where(valid, score, jnp.inf), axis=-1, keepdims=True))
        hi = jnp.maximum(hi, jnp.max(jnp.where(valid, score, NEG_INF), axis=-1, keepdims=True))
        return lo, hi

    lo0, hi0 = lax.fori_loop(0, n_kc, score_chunk,
                             (jnp.full((tq, 1), jnp.inf, F32), jnp.full((tq, 1), NEG_INF, F32)))

    lane_reps = tk // LANES

    def count_ge(thr):
        thr_b = jnp.tile(jnp.broadcast_to(thr, (tq, LANES)), (1, lane_reps))

        def body(c, acc):
            ks = pl.multiple_of(c * tk, tk)
            hit = jnp.where(sc_ref[:, pl.ds(ks, tk)] >= thr_b, 1.0, 0.0)
            for r in range(lane_reps):
                acc = acc + hit[:, r * LANES:(r + 1) * LANES]
            return acc
        partial = lax.fori_loop(0, n_kc, body, jnp.zeros((tq, LANES), F32))
        return jnp.sum(partial, axis=-1, keepdims=True)

    k_f = float(top_k)
    n_keys = (qpos + 1).astype(F32)

    def unsettled(cnt_lo):
        return jnp.max(jnp.where(cnt_lo > k_f, 1.0, 0.0)) > 0.0

    def cond(state):
        it, _, _, cnt_lo = state
        return jnp.logical_and(it < max_iters, unsettled(cnt_lo))

    def refine(state):
        it, lo, hi, cnt_lo = state
        mid = 0.5 * (lo + hi)
        cnt = count_ge(mid)
        keep = cnt >= k_f
        lo = jnp.where(keep, mid, lo)
        cnt_lo = jnp.where(keep, cnt, cnt_lo)
        hi = jnp.where(keep, hi, mid)
        return it + 1, lo, hi, cnt_lo

    _, thr, _, _ = lax.while_loop(cond, refine, (jnp.int32(0), lo0, hi0, n_keys))

    m_ref[...] = jnp.full_like(m_ref, MASK_BIAS)
    l_ref[...] = jnp.zeros_like(l_ref)
    acc_ref[...] = jnp.zeros_like(acc_ref)
    c_exp = DSA_DH ** -0.5 * LOG2_E
    thr_b = jnp.tile(jnp.broadcast_to(thr, (tq, LANES)), (1, lane_reps))

    def attend(c, carry):
        ks = pl.multiple_of(c * tk, tk)
        bias = jnp.where(sc_ref[:, pl.ds(ks, tk)] >= thr_b, 0.0, MASK_BIAS)
        for g in range(DSA_KV_HEADS):
            heads = slice(g * group, (g + 1) * group)
            qg = jnp.concatenate(
                [q_ref[:, (g * group + j) * DSA_DH:(g * group + j + 1) * DSA_DH] for j in range(group)], axis=0)
            kc = k_ref[pl.ds(ks, tk), g * DSA_DH:(g + 1) * DSA_DH]
            vc = v_ref[pl.ds(ks, tk), g * DSA_DH:(g + 1) * DSA_DH]
            s = _nt_dot(qg, kc).reshape(group, tq, tk) + bias[None]
            m_prev = m_ref[heads]
            m_new = jnp.maximum(m_prev, jnp.max(s, axis=-1, keepdims=True))
            p = jnp.exp2((s - jnp.tile(m_new, (1, 1, lane_reps))) * c_exp)
            alpha = jnp.exp2((m_prev - m_new) * c_exp)
            p_sum = p[:, :, 0:LANES]
            for r in range(1, lane_reps):
                p_sum = p_sum + p[:, :, r * LANES:(r + 1) * LANES]
            l_ref[heads] = alpha * l_ref[heads] + p_sum
            pv = jnp.dot(p.reshape(group * tq, tk).astype(BF16), vc, preferred_element_type=F32)
            acc_ref[heads] = alpha * acc_ref[heads] + pv.reshape(group, tq, DSA_DH)
            m_ref[heads] = m_new
        return carry

    lax.fori_loop(0, n_kc, attend, 0)
    for h in range(DSA_Q_HEADS):
        l = jnp.sum(l_ref[h], axis=-1, keepdims=True)
        o_ref[:, h * DSA_DH:(h + 1) * DSA_DH] = (acc_ref[h] / l).astype(o_ref.dtype)


def _dsa(qk, v, iq_hm, small, batch, seq_len, tq=128, tk=512):
    t = qk.shape[0]
    tk = min(tk, seq_len)
    nq = seq_len // tq
    qw = DSA_Q_HEADS * DSA_DH
    kw = DSA_KV_HEADS * DSA_DH
    top_k = min(DSA_TOPK_MAX, seq_len // 4)
    return pl.pallas_call(
        functools.partial(_dsa_kernel, tq=tq, tk=tk, top_k=top_k, max_iters=64),
        grid=(batch, nq),
        in_specs=[
            pl.BlockSpec((tq, qw), lambda b, i: (b * nq + i, 0)),
            pl.BlockSpec((seq_len, kw), lambda b, i: (b, qw // kw)),
            pl.BlockSpec((seq_len, kw), lambda b, i: (b, 0)),
            pl.BlockSpec((1, IDX_HEADS, tq, IDX_DH), lambda b, i: (b, 0, i, 0)),
            pl.BlockSpec((seq_len, LANES), lambda b, i: (b, 0)),
            pl.BlockSpec((tq, LANES), lambda b, i: (b * nq + i, 0)),
        ],
        out_specs=pl.BlockSpec((tq, qw), lambda b, i: (b * nq + i, 0)),
        out_shape=jax.ShapeDtypeStruct((t, qw), BF16),
        scratch_shapes=[
            pltpu.VMEM((tq, seq_len), F32),
            pltpu.VMEM((DSA_Q_HEADS, tq, LANES), F32),
            pltpu.VMEM((DSA_Q_HEADS, tq, LANES), F32),
            pltpu.VMEM((DSA_Q_HEADS, tq, DSA_DH), F32),
        ],
        compiler_params=_params("parallel", "arbitrary"),
        name="dsa",
    )(qk, qk, v, iq_hm, small, small)


def _router_kernel(x_ref, w_ref, b_ref, o_ref):
    x = x_ref[...]
    w = w_ref[...]
    x_hi = x.astype(BF16)
    w_hi = w.astype(BF16)
    x_lo = (x - x_hi.astype(F32)).astype(BF16)
    w_lo = (w - w_hi.astype(F32)).astype(BF16)
    logits = (jnp.dot(x_hi, w_hi, preferred_element_type=F32)
              + (jnp.dot(x_hi, w_lo, preferred_element_type=F32) + jnp.dot(x_lo, w_hi, preferred_element_type=F32))
              + b_ref[...])
    lane = lax.broadcasted_iota(jnp.int32, logits.shape, 1)
    big = jnp.int32(LANES)
    g_logits = jnp.where(lane < N_GROUPS, logits, NEG_INF)
    g_max = jnp.max(g_logits, axis=-1, keepdims=True)
    g_prob = 1.0 / jnp.sum(jnp.exp(g_logits - g_max), axis=-1, keepdims=True)
    g_idx = jnp.min(jnp.where(g_logits == g_max, lane, big), axis=-1, keepdims=True)
    first = N_GROUPS + g_idx * EXPERTS_PER_GROUP
    e_logits = jnp.where((lane >= first) & (lane < first + EXPERTS_PER_GROUP), logits, NEG_INF)
    v1 = jnp.max(e_logits, axis=-1, keepdims=True)
    i1 = jnp.min(jnp.where(e_logits == v1, lane, big), axis=-1, keepdims=True)
    rest = jnp.where(lane == i1, NEG_INF, e_logits)
    v2 = jnp.max(rest, axis=-1, keepdims=True)
    i2 = jnp.min(jnp.where(rest == v2, lane, big), axis=-1, keepdims=True)
    e2 = jnp.exp(v2 - v1)
    w1 = (1.0 / (1.0 + e2)) * g_prob
    w2 = (e2 / (1.0 + e2)) * g_prob
    out = jnp.where(lane == 0, (i1 - N_GROUPS).astype(F32), 0.0)
    out = jnp.where(lane == 1, (i2 - N_GROUPS).astype(F32), out)
    out = jnp.where(lane == 2, w1, out)
    out = jnp.where(lane == 3, w2, out)
    o_ref[...] = out


def _router(x, w_cat, b_cat, tm=512):
    t, d = x.shape
    return pl.pallas_call(
        _router_kernel,
        grid=(t // tm,),
        in_specs=[pl.BlockSpec((tm, d), lambda i: (i, 0)),
                  pl.BlockSpec((d, LANES), lambda i: (0, 0)),
                  pl.BlockSpec((1, LANES), lambda i: (0, 0))],
        out_specs=pl.BlockSpec((tm, LANES), lambda i: (i, 0)),
        out_shape=jax.ShapeDtypeStruct((t, LANES), F32),
        compiler_params=_params("parallel"),
        name="moe_router",
    )(x, w_cat, b_cat)


def _plan_kernel(pos_ref, src_ref, dst_ref, *, n_tokens):
    def body(a, c):
        p = pos_ref[a]
        src_ref[p] = jnp.where(a >= n_tokens, a - n_tokens, a)
        dst_ref[p] = a
        return c

    lax.fori_loop(0, 2 * n_tokens, body, 0, unroll=8)


def _plan(pos, n_tokens):
    smem = pl.BlockSpec(memory_space=pltpu.SMEM)
    return pl.pallas_call(
        functools.partial(_plan_kernel, n_tokens=n_tokens),
        in_specs=[smem],
        out_specs=[smem, smem],
        out_shape=[jax.ShapeDtypeStruct((2 * n_tokens,), jnp.int32)] * 2,
        name="moe_plan",
    )(pos)


def _for_rows(n, tm, fn):
    def body(i, c):
        fn(i)
        return c

    @pl.when(n == tm)
    def _():
        lax.fori_loop(0, tm, body, 0, unroll=8)

    @pl.when(n < tm)
    def _():
        lax.fori_loop(0, n, body, 0)


def _experts_kernel(tile_e_ref, tile_start_ref, tile_n_ref, src_ref, dst_ref,
                    x_hbm, wg_ref, wu_ref, wd_ref, y_hbm, xbuf, ybuf, gsem, ssem, *, tm, n_tiles):
    del tile_e_ref
    t = pl.program_id(0)
    slot = t % 2

    def gather_row(tile, s, i):
        row = src_ref[tile_start_ref[tile] + i]
        return pltpu.make_async_copy(x_hbm.at[pl.ds(row, 1)], xbuf.at[s, pl.ds(i, 1)], gsem.at[s])

    def scatter_row(tile, s, i):
        row = dst_ref[tile_start_ref[tile] + i]
        return pltpu.make_async_copy(ybuf.at[s, pl.ds(i, 1)], y_hbm.at[pl.ds(row, 1)], ssem.at[s])

    def start_gather(tile, s):
        _for_rows(tile_n_ref[tile], tm, lambda i: gather_row(tile, s, i).start())

    def wait_rows(tile, bulk_copy, row_copy):
        n = tile_n_ref[tile]
        n8 = pl.multiple_of(n // SUBLANES * SUBLANES, SUBLANES)

        @pl.when(n8 > 0)
        def _():
            bulk_copy(n8).wait()

        def tail(i, c):
            row_copy(i).wait()
            return c
        lax.fori_loop(n8, n, tail, 0)

    def wait_gather(tile, s):
        wait_rows(tile,
                  lambda n8: pltpu.make_async_copy(x_hbm.at[pl.ds(0, n8)], xbuf.at[s, pl.ds(0, n8)], gsem.at[s]),
                  lambda i: gather_row(tile, s, i))

    def wait_scatter(tile, s):
        wait_rows(tile,
                  lambda n8: pltpu.make_async_copy(ybuf.at[s, pl.ds(0, n8)], y_hbm.at[pl.ds(0, n8)], ssem.at[s]),
                  lambda i: scatter_row(tile, s, i))

    @pl.when(t == 0)
    def _():
        xbuf[...] = jnp.zeros_like(xbuf)
        start_gather(0, 0)

    @pl.when(t + 1 < n_tiles)
    def _():
        start_gather(t + 1, 1 - slot)

    @pl.when(t >= 2)
    def _():
        wait_scatter(t - 2, slot)

    n_rows = tile_n_ref[t]

    @pl.when(n_rows > 0)
    def _():
        wait_gather(t, slot)
        xb = xbuf[slot].astype(BF16)
        gate = jnp.dot(xb, wg_ref[0, 0, 0].astype(BF16), preferred_element_type=F32)
        up = jnp.dot(xb, wu_ref[0, 0, 0].astype(BF16), preferred_element_type=F32)
        hidden = (_silu(gate) * up).astype(BF16)
        ybuf[slot] = jnp.dot(hidden, wd_ref[0, 0, 0].astype(BF16), preferred_element_type=F32)
        _for_rows(n_rows, tm, lambda i: scatter_row(t, slot, i).start())

    @pl.when(t == n_tiles - 1)
    def _():
        if n_tiles >= 2:
            wait_scatter(t - 1, 1 - slot)
        wait_scatter(t, slot)


def _experts(x, w_gate, w_up, w_down, layer, tile_e, tile_start, tile_n, src_rows, dst_rows, tm, n_tiles):
    t_tokens, d = x.shape
    ff = w_gate.shape[-1]
    w_map = lambda t, te, ts, tn, sr, ds_: (layer, te[t] // EXPERTS_PER_GROUP, te[t] % EXPERTS_PER_GROUP, 0, 0)
    grid_spec = pltpu.PrefetchScalarGridSpec(
        num_scalar_prefetch=5,
        grid=(n_tiles,),
        in_specs=[
            pl.BlockSpec(memory_space=pl.ANY),
            pl.BlockSpec((1, 1, 1, d, ff), w_map),
            pl.BlockSpec((1, 1, 1, d, ff), w_map),
            pl.BlockSpec((1, 1, 1, ff, d), w_map),
        ],
        out_specs=pl.BlockSpec(memory_space=pl.ANY),
        scratch_shapes=[
            pltpu.VMEM((2, tm, d), F32),
            pltpu.VMEM((2, tm, d), F32),
            pltpu.SemaphoreType.DMA((2,)),
            pltpu.SemaphoreType.DMA((2,)),
        ],
    )
    return pl.pallas_call(
        functools.partial(_experts_kernel, tm=tm, n_tiles=n_tiles),
        grid_spec=grid_spec,
        out_shape=jax.ShapeDtypeStruct((2 * t_tokens, d), F32),
        compiler_params=_params("arbitrary"),
        name="moe_experts",
    )(tile_e, tile_start, tile_n, src_rows, dst_rows, x, w_gate, w_up, w_down)


def _combine_ln_kernel(x_ref, y0_ref, y1_ref, r_ref, g_ref, b_ref, o_ref, o_bf_ref):
    w0 = r_ref[:, 2:3]
    w1 = r_ref[:, 3:4]
    y = DN_ALPHA * x_ref[...] + (w0 * y0_ref[...] + w1 * y1_ref[...])
    out = _layer_norm(y, g_ref[...], b_ref[...])
    o_ref[...] = out
    o_bf_ref[...] = out.astype(BF16)


def _combine_ln(x, y2, route, g, b, tm=512):
    t, d = x.shape
    n_t = t // tm
    return pl.pallas_call(
        _combine_ln_kernel,
        grid=(n_t,),
        in_specs=[
            pl.BlockSpec((tm, d), lambda i: (i, 0)),
            pl.BlockSpec((tm, d), lambda i: (i, 0)),
            pl.BlockSpec((tm, d), lambda i: (i + n_t, 0)),
            pl.BlockSpec((tm, LANES), lambda i: (i, 0)),
            pl.BlockSpec((1, d), lambda i: (0, 0)),
            pl.BlockSpec((1, d), lambda i: (0, 0)),
        ],
        out_specs=[pl.BlockSpec((tm, d), lambda i: (i, 0)), pl.BlockSpec((tm, d), lambda i: (i, 0))],
        out_shape=[jax.ShapeDtypeStruct((t, d), F32), jax.ShapeDtypeStruct((t, d), BF16)],
        compiler_params=_params("parallel"),
        name="moe_combine_ln",
    )(x, y2, y2, route, g.reshape(1, d), b.reshape(1, d))


def _moe_layer(x, wg, bg, we, be, w_gate, w_up, w_down, layer, ln_g, ln_b, tm=256):
    t, d = x.shape
    w_cat = jnp.concatenate([wg, jnp.transpose(we, (1, 0, 2)).reshape(d, N_EXPERTS)], axis=1)
    w_cat = jnp.pad(w_cat, ((0, 0), (0, LANES - w_cat.shape[1])))
    b_cat = jnp.pad(jnp.concatenate([bg, be.reshape(-1)]), (0, LANES - N_GROUPS - N_EXPERTS)).reshape(1, LANES)
    route = _router(x, w_cat, b_cat)

    i32 = jnp.int32
    expert = jnp.concatenate([route[:, 0], route[:, 1]]).astype(i32)
    e_ids = jnp.arange(N_EXPERTS, dtype=i32)
    onehot = (expert[:, None] == e_ids[None, :]).astype(i32)
    running = jnp.cumsum(onehot, axis=0)
    counts = running[-1]
    seg_start = jnp.cumsum(counts) - counts
    pos = jnp.sum((running - onehot + seg_start[None, :]) * onehot, axis=1)
    tiles_per = (counts + tm - 1) // tm
    tile_end = jnp.cumsum(tiles_per)
    tile_begin = tile_end - tiles_per
    n_tiles = (2 * t) // tm + N_EXPERTS
    tile_ids = jnp.arange(n_tiles, dtype=i32)
    n_used = tile_end[-1]
    tile_e = jnp.sum((tile_end[None, :] <= jnp.minimum(tile_ids, n_used - 1)[:, None]).astype(i32), axis=1)
    tile_e = jnp.clip(tile_e, 0, N_EXPERTS - 1)
    te_onehot = (tile_e[:, None] == e_ids[None, :]).astype(i32)
    within = (tile_ids - jnp.sum(te_onehot * tile_begin[None, :], axis=1)) * tm
    tile_n = jnp.clip(jnp.sum(te_onehot * counts[None, :], axis=1) - within, 0, tm)
    tile_n = jnp.where(tile_ids < n_used, tile_n, 0).astype(i32)
    tile_start = jnp.where(tile_n > 0, jnp.sum(te_onehot * seg_start[None, :], axis=1) + within, 0).astype(i32)
    src_rows, dst_rows = _plan(pos.astype(i32), t)
    y2 = _experts(x, w_gate, w_up, w_down, layer, tile_e.astype(i32), tile_start, tile_n, src_rows, dst_rows,
                  tm, n_tiles)
    return _combine_ln(x, y2, route, ln_g, ln_b)


def _even_mixer_layer(x, xb, batch, seq_len, w_in, gate_w2, gate_b, norm_g, fox_gate_b, w_out, ln_g, ln_b):
    gla_w = 2 * GLA_HEADS * GLA_DK + 2 * GLA_HEADS * GLA_DV
    fox_w = 3 * FOX_HEADS * FOX_DH
    glr0 = gla_w
    fox0 = glr0 + GLA_GATE_RANK
    ff0 = fox0 + fox_w
    main = _project(xb, w_in, 0, gla_w, F32, seq_len)
    fqkv = _project(xb, w_in[:, fox0:ff0], 0, fox_w, BF16, seq_len)
    w_small = jnp.concatenate([w_in[:, ff0:ff0 + FOX_HEADS], w_in[:, glr0:fox0]], axis=1)
    w_small = jnp.pad(w_small, ((0, 0), (0, LANES - w_small.shape[1])))
    small = _project(xb, w_small, 0, LANES, F32, seq_len)
    o_gla = _gla(main, small, gate_w2, gate_b, norm_g, batch, seq_len)
    gate_b_row = jnp.pad(fox_gate_b, (0, LANES - FOX_HEADS)).reshape(1, LANES)
    c, c_t = _fox_gate(small, gate_b_row, batch, seq_len)
    fox_heads_per_step = 4
    cq = c.reshape(batch, seq_len, LANES)[:, :, :FOX_HEADS]
    cq = jnp.transpose(cq.reshape(batch, seq_len, FOX_HEADS // fox_heads_per_step, fox_heads_per_step), (0, 2, 1, 3))
    ck = c_t[:, :FOX_HEADS, :].reshape(batch * FOX_HEADS, 1, seq_len)
    o_fox = _fox(fqkv, cq, ck, batch, seq_len, heads=fox_heads_per_step)
    return _out_proj_ln(o_gla, o_fox, 0, _to_bf16(w_out), x, ln_g, ln_b)


def _odd_mixer_layer(x, xb, batch, seq_len, w_in, w_out, ln_g, ln_b):
    qw = DSA_Q_HEADS * DSA_DH
    kw = DSA_KV_HEADS * DSA_DH
    iqw = IDX_HEADS * IDX_DH
    rope_attn = (DSA_DH,) + _rope_tables(seq_len, DSA_DH)
    rope_idx = (IDX_DH,) + _rope_tables(seq_len, IDX_DH)
    qk = _project(xb, w_in, 0, qw + kw, BF16, seq_len, rope=rope_attn)
    v = _project(xb, w_in, qw + kw, kw, BF16, seq_len)
    iq = _project(xb, w_in, qw + 2 * kw, iqw, BF16, seq_len, rope=rope_idx)
    small0 = qw + 2 * kw + iqw
    w_small = jnp.pad(w_in[:, small0:], ((0, 0), (0, LANES - (w_in.shape[1] - small0))))
    small = _project(xb, w_small, 0, LANES, F32, seq_len, rope=rope_idx, rope_lanes=IDX_DH)
    iq_hm = jnp.transpose(iq.reshape(batch, seq_len, IDX_HEADS, IDX_DH), (0, 2, 1, 3))
    o = _dsa(qk, v, iq_hm, small, batch, seq_len)
    return _out_proj_ln(o, o, 1, _to_bf16(w_out), x, ln_g, ln_b)


def kernel(x, a_w_in, a_gla_gate_w2, a_gla_gate_b, a_gla_norm_g, a_fox_gate_b, a_w_out, c_w_in, c_w_out,
           ln_mix_g, ln_mix_b, ln_ffn_g, ln_ffn_b, moe_group_w, moe_group_b, moe_expert_w, moe_expert_b,
           moe_w_gate, moe_w_up, moe_w_down):
    batch, seq_len, d = x.shape
    h = x.reshape(batch * seq_len, d)
    hb = _to_bf16(h)
    for layer in range(DEPTH):
        j = layer // 2
        if layer % 2 == 0:
            h, hb = _even_mixer_layer(h, hb, batch, seq_len, a_w_in[j], a_gla_gate_w2[j], a_gla_gate_b[j],
                                      a_gla_norm_g[j], a_fox_gate_b[j], a_w_out[j],
                                      ln_mix_g[layer], ln_mix_b[layer])
        else:
            h, hb = _odd_mixer_layer(h, hb, batch, seq_len, c_w_in[j], c_w_out[j],
                                     ln_mix_g[layer], ln_mix_b[layer])
        h, hb = _moe_layer(h, moe_group_w[layer], moe_group_b[layer], moe_expert_w[layer], moe_expert_b[layer],
                           moe_w_gate, moe_w_up, moe_w_down, layer, ln_ffn_g[layer], ln_ffn_b[layer])
    return h.reshape(batch, seq_len, d)
```

```python
import functools

import jax
import jax.numpy as jnp
import numpy as np
from jax import lax
from jax.experimental import pallas as pl
from jax.experimental.pallas import tpu as pltpu

F32 = jnp.float32
BF16 = jnp.bfloat16

DEPTH = 2
GLA_HEADS = 4
GLA_DK = 128
GLA_DV = 256
GLA_GATE_RANK = 16
GLA_GATE_TAU = 16.0
GLA_CHUNK = 64
FOX_HEADS = 8
FOX_DH = 128
DSA_Q_HEADS = 16
DSA_KV_HEADS = 4
DSA_DH = 128
IDX_HEADS = 16
IDX_DH = 64
DSA_TOPK_MAX = 256
N_GROUPS = 4
EXPERTS_PER_GROUP = 8
N_EXPERTS = N_GROUPS * EXPERTS_PER_GROUP
ROPE_THETA = 10000.0
LN_EPS = 1e-5
DN_ALPHA = (2 * DEPTH) ** 0.25

LANES = 128
SUBLANES = 8
VMEM_LIMIT_BYTES = 56 * 1024 * 1024

NEG_INF = float("-inf")
LOG2_E = 1.4426950408889634
MASK_BIAS = -1e30


def _params(*semantics):
    return pltpu.CompilerParams(dimension_semantics=semantics, vmem_limit_bytes=VMEM_LIMIT_BYTES)


def _nt_dot(a, b):
    return lax.dot_general(a, b, (((1,), (1,)), ((), ())), preferred_element_type=F32)


def _tn_dot(a, b):
    return lax.dot_general(a, b, (((0,), (0,)), ((), ())), preferred_element_type=F32)


def _log_sigmoid(z):
    return jnp.minimum(z, 0.0) - jnp.log1p(jnp.exp(-jnp.abs(z)))


def _silu(z):
    return z * (1.0 / (1.0 + jnp.exp(-z)))


def _cumsum_rows(x):
    n = x.shape[0]
    row = lax.broadcasted_iota(jnp.int32, x.shape, 0)
    shift = 1
    while shift < n:
        x = x + jnp.where(row >= shift, pltpu.roll(x, shift, 0), 0.0)
        shift *= 2
    return x


def _rot_half(x, head_dim):
    half = head_dim // 2
    n = x.shape[-1]
    lane = lax.broadcasted_iota(jnp.int32, x.shape, 1)
    fwd = pltpu.roll(x, n - half, 1)
    bwd = pltpu.roll(x, half, 1)
    return jnp.where((lane % head_dim) < half, fwd, bwd)


def _layer_norm(y, g, b):
    mu = jnp.mean(y, axis=-1, keepdims=True)
    d = y - mu
    var = jnp.mean(d * d, axis=-1, keepdims=True)
    return d * lax.rsqrt(var + LN_EPS) * g + b


def _cast_kernel(x_ref, o_ref):
    o_ref[...] = x_ref[...].astype(o_ref.dtype)


def _to_bf16(a, rows=512):
    r, c = a.shape
    rows = min(rows, r)
    return pl.pallas_call(
        _cast_kernel,
        grid=(r // rows,),
        in_specs=[pl.BlockSpec((rows, c), lambda i: (i, 0))],
        out_specs=pl.BlockSpec((rows, c), lambda i: (i, 0)),
        out_shape=jax.ShapeDtypeStruct((r, c), BF16),
        compiler_params=_params("parallel"),
        name="to_bf16",
    )(a)


def _proj_kernel(*refs, rope_dim, rope_lanes):
    if rope_dim:
        x_ref, w_ref, cos_ref, sin_ref, o_ref = refs
    else:
        x_ref, w_ref, o_ref = refs
    acc = jnp.dot(x_ref[...], w_ref[...].astype(BF16), preferred_element_type=F32)
    if rope_dim:
        reps = acc.shape[-1] // LANES
        cos = jnp.tile(cos_ref[...], (1, reps))
        sin = jnp.tile(sin_ref[...], (1, reps))
        roped = acc * cos + _rot_half(acc, rope_dim) * sin
        if rope_lanes < acc.shape[-1]:
            lane = lax.broadcasted_iota(jnp.int32, acc.shape, 1)
            roped = jnp.where(lane < rope_lanes, roped, acc)
        acc = roped
    o_ref[...] = acc.astype(o_ref.dtype)


def _project(x, w, col_start, n_cols, out_dtype, seq_len, rope=None, rope_lanes=None, tm=1024, tn=512):
    t, k = x.shape
    tm = min(tm, seq_len)
    tn = min(tn, n_cols)
    assert t % tm == 0 and n_cols % tn == 0 and col_start % tn == 0 and seq_len % tm == 0
    col_off = col_start // tn
    in_specs = [
        pl.BlockSpec((tm, k), lambda i, j: (i, 0)),
        pl.BlockSpec((k, tn), lambda i, j: (0, j + col_off)),
    ]
    args = [x, w]
    rope_dim = 0
    if rope is not None:
        rope_dim, cos, sin = rope
        seq_tiles = seq_len // tm
        in_specs += [pl.BlockSpec((tm, LANES), lambda i, j: (i % seq_tiles, 0))] * 2
        args += [cos, sin]
    return pl.pallas_call(
        functools.partial(_proj_kernel, rope_dim=rope_dim, rope_lanes=rope_lanes or n_cols),
        grid=(t // tm, n_cols // tn),
        in_specs=in_specs,
        out_specs=pl.BlockSpec((tm, tn), lambda i, j: (i, j)),
        out_shape=jax.ShapeDtypeStruct((t, n_cols), out_dtype),
        compiler_params=_params("parallel", "parallel"),
        name="in_proj",
    )(*args)


def _rope_tables(seq_len, head_dim):
    half = head_dim // 2
    inv = ROPE_THETA ** (-jnp.arange(half, dtype=F32) / half)
    ang = jnp.arange(seq_len, dtype=F32)[:, None] * inv[None, :]
    cos, sin = jnp.cos(ang), jnp.sin(ang)
    reps = LANES // head_dim
    cos_full = jnp.tile(jnp.concatenate([cos, cos], axis=-1), (1, reps))
    sin_signed = jnp.tile(jnp.concatenate([-sin, sin], axis=-1), (1, reps))
    return cos_full, sin_signed


def _out_proj_ln_kernel(oa_ref, ob_ref, w_ref, x_ref, g_ref, b_ref, out_ref, out_bf_ref):
    half = oa_ref.shape[1]
    h = jnp.dot(oa_ref[...], w_ref[0:half, :], preferred_element_type=F32)
    h = h + jnp.dot(ob_ref[...], w_ref[half:2 * half, :], preferred_element_type=F32)
    out = _layer_norm(DN_ALPHA * x_ref[...] + h, g_ref[...], b_ref[...])
    out_ref[...] = out
    out_bf_ref[...] = out.astype(BF16)


def _out_proj_ln(o_a, o_b, b_col_block, w_bf16, x, g, b, tm=512):
    t = o_a.shape[0]
    kdim, d = w_bf16.shape
    half = kdim // 2
    tm = min(tm, t)
    return pl.pallas_call(
        _out_proj_ln_kernel,
        grid=(t // tm,),
        in_specs=[
            pl.BlockSpec((tm, half), lambda i: (i, 0)),
            pl.BlockSpec((tm, half), lambda i: (i, b_col_block)),
            pl.BlockSpec((kdim, d), lambda i: (0, 0)),
            pl.BlockSpec((tm, d), lambda i: (i, 0)),
            pl.BlockSpec((1, d), lambda i: (0, 0)),
            pl.BlockSpec((1, d), lambda i: (0, 0)),
        ],
        out_specs=[pl.BlockSpec((tm, d), lambda i: (i, 0)), pl.BlockSpec((tm, d), lambda i: (i, 0))],
        out_shape=[jax.ShapeDtypeStruct((t, d), F32), jax.ShapeDtypeStruct((t, d), BF16)],
        compiler_params=_params("parallel"),
        name="out_proj_ln",
    )(o_a, o_b, w_bf16, x, g.reshape(1, d), b.reshape(1, d))


def _gla_kernel(q_ref, k_ref, v_ref, r_ref, sm_ref, w2_ref, gb_ref, ng_ref, o_ref, state_ref, *, n_chunks):
    c_len, dk, dv = GLA_CHUNK, GLA_DK, GLA_DV

    @pl.when(pl.program_id(1) == 0)
    def _():
        state_ref[...] = jnp.zeros_like(state_ref)

    row = lax.broadcasted_iota(jnp.int32, (c_len, c_len), 0)
    col = lax.broadcasted_iota(jnp.int32, (c_len, c_len), 1)
    causal = row >= col
    norm_g = ng_ref[...]
    for c in range(n_chunks):
        rows = slice(c * c_len, (c + 1) * c_len)
        glr = sm_ref[rows, FOX_HEADS:FOX_HEADS + GLA_GATE_RANK].astype(BF16)
        for h in range(GLA_HEADS):
            kcols = slice(h * dk, (h + 1) * dk)
            vcols = slice(h * dv, (h + 1) * dv)
            z = jnp.dot(glr, w2_ref[:, kcols].astype(BF16), preferred_element_type=F32) + gb_ref[:, kcols]
            log_a = _log_sigmoid(z) / GLA_GATE_TAU
            b = _cumsum_rows(log_a)
            b_last = b[c_len - 1:c_len, :]
            q = q_ref[rows, kcols] * (dk ** -0.5)
            k = k_ref[rows, kcols]
            v = v_ref[rows, vcols].astype(BF16)
            q_dec = (q * jnp.exp(b)).astype(BF16)
            k_inv = (k * jnp.exp(-b)).astype(BF16)
            k_end = (k * jnp.exp(b_last - b)).astype(BF16)
            scores = jnp.where(causal, _nt_dot(q_dec, k_inv), 0.0)
            state_t = state_ref[h]
            o = jnp.dot(scores.astype(BF16), v, preferred_element_type=F32)
            o = o + _nt_dot(q_dec, state_t.astype(BF16))
            state_ref[h] = jnp.exp(b_last) * state_t + _tn_dot(v, k_end)
            rms = lax.rsqrt(jnp.mean(o * o, axis=-1, keepdims=True) + LN_EPS)
            o_ref[rows, vcols] = (o * rms * norm_g * _silu(r_ref[rows, vcols])).astype(o_ref.dtype)


def _gla(main, small, gate_w2, gate_b, norm_g, batch, seq_len, rows_per_step=256):
    t = main.shape[0]
    n_steps = seq_len // rows_per_step
    hk, hv = GLA_HEADS * GLA_DK, GLA_HEADS * GLA_DV
    row_map = lambda b, n: b * n_steps + n
    return pl.pallas_call(
        functools.partial(_gla_kernel, n_chunks=rows_per_step // GLA_CHUNK),
        grid=(batch, n_steps),
        in_specs=[
            pl.BlockSpec((rows_per_step, hk), lambda b, n: (row_map(b, n), 0)),
            pl.BlockSpec((rows_per_step, hk), lambda b, n: (row_map(b, n), 1)),
            pl.BlockSpec((rows_per_step, hv), lambda b, n: (row_map(b, n), 1)),
            pl.BlockSpec((rows_per_step, hv), lambda b, n: (row_map(b, n), 2)),
            pl.BlockSpec((rows_per_step, LANES), lambda b, n: (row_map(b, n), 0)),
            pl.BlockSpec((GLA_GATE_RANK, hk), lambda b, n: (0, 0)),
            pl.BlockSpec((1, hk), lambda b, n: (0, 0)),
            pl.BlockSpec((1, GLA_DV), lambda b, n: (0, 0)),
        ],
        out_specs=pl.BlockSpec((rows_per_step, hv), lambda b, n: (row_map(b, n), 0)),
        out_shape=jax.ShapeDtypeStruct((t, hv), BF16),
        scratch_shapes=[pltpu.VMEM((GLA_HEADS, GLA_DV, GLA_DK), F32)],
        compiler_params=_params("parallel", "arbitrary"),
        name="gla",
    )(main, main, main, main, small, gate_w2, gate_b.reshape(1, hk), norm_g.reshape(1, GLA_DV))


def _fox_gate_kernel(sm_ref, b_ref, c_ref, ct_ref):
    c = _cumsum_rows(_log_sigmoid(sm_ref[...] + b_ref[...]))
    c_ref[...] = c
    ct_ref[0] = c.T


def _fox_gate(small, gate_b_row, batch, seq_len):
    t = small.shape[0]
    return pl.pallas_call(
        _fox_gate_kernel,
        grid=(batch,),
        in_specs=[pl.BlockSpec((seq_len, LANES), lambda b: (b, 0)), pl.BlockSpec((1, LANES), lambda b: (0, 0))],
        out_specs=[pl.BlockSpec((seq_len, LANES), lambda b: (b, 0)),
                   pl.BlockSpec((1, LANES, seq_len), lambda b: (b, 0, 0))],
        out_shape=[jax.ShapeDtypeStruct((t, LANES), F32), jax.ShapeDtypeStruct((batch, LANES, seq_len), F32)],
        compiler_params=_params("parallel"),
        name="fox_gate",
    )(small, gate_b_row)


def _fox_kernel(q_ref, k_ref, v_ref, cq_ref, ck_ref, o_ref, m_ref, l_ref, acc_ref, *, tq, heads):
    qi = pl.program_id(2)
    dh = FOX_DH
    scale = dh ** -0.5
    inv_scale = dh ** 0.5
    c_exp = scale * LOG2_E
    reps = tq // LANES

    m_ref[...] = jnp.full_like(m_ref, MASK_BIAS)
    l_ref[...] = jnp.zeros_like(l_ref)
    acc_ref[...] = jnp.zeros_like(acc_ref)

    def step(j, masked):
        ks = pl.multiple_of(j * tq, tq)
        for h in range(heads):
            cols = slice(h * dh, (h + 1) * dh)
            q = q_ref[:, cols]
            k = k_ref[pl.ds(ks, tq), cols]
            v = v_ref[pl.ds(ks, tq), cols]
            cq = cq_ref[0, 0, :, h:h + 1] * inv_scale
            s = _nt_dot(q, k) - ck_ref[h, :, pl.ds(ks, tq)] * inv_scale
            if masked:
                row = lax.broadcasted_iota(jnp.int32, s.shape, 0)
                col = lax.broadcasted_iota(jnp.int32, s.shape, 1)
                s = jnp.where(col <= row, s, MASK_BIAS)
            m_prev = m_ref[h]
            m_new = jnp.maximum(m_prev, jnp.max(s, axis=-1, keepdims=True) + cq)
            shift = m_new - cq
            p = jnp.exp2((s - jnp.tile(shift, (1, reps))) * c_exp)
            alpha = jnp.exp2((m_prev - m_new) * c_exp)
            p_sum = p[:, 0:LANES]
            for r in range(1, reps):
                p_sum = p_sum + p[:, r * LANES:(r + 1) * LANES]
            l_ref[h] = alpha * l_ref[h] + p_sum
            acc_ref[h] = alpha * acc_ref[h] + jnp.dot(p.astype(BF16), v, preferred_element_type=F32)
            m_ref[h] = m_new

    def body(j, carry):
        step(j, False)
        return carry

    lax.fori_loop(0, qi, body, 0)
    step(qi, True)
    for h in range(heads):
        l = jnp.sum(l_ref[h], axis=-1, keepdims=True)
        o_ref[:, h * dh:(h + 1) * dh] = (acc_ref[h] / l).astype(o_ref.dtype)


def _fox(fqkv, cq, ck, batch, seq_len, tq=512, heads=4):
    t = fqkv.shape[0]
    nq = seq_len // tq
    n_hg = FOX_HEADS // heads
    w = heads * FOX_DH
    return pl.pallas_call(
        functools.partial(_fox_kernel, tq=tq, heads=heads),
        grid=(batch, n_hg, nq),
        in_specs=[
            pl.BlockSpec((tq, w), lambda b, g, i: (b * nq + i, g)),
            pl.BlockSpec((seq_len, w), lambda b, g, i: (b, n_hg + g)),
            pl.BlockSpec((seq_len, w), lambda b, g, i: (b, 2 * n_hg + g)),
            pl.BlockSpec((1, 1, tq, heads), lambda b, g, i: (b, g, i, 0)),
            pl.BlockSpec((heads, 1, seq_len), lambda b, g, i: (b * n_hg + g, 0, 0)),
        ],
        out_specs=pl.BlockSpec((tq, w), lambda b, g, i: (b * nq + i, g)),
        out_shape=jax.ShapeDtypeStruct((t, FOX_HEADS * FOX_DH), BF16),
        scratch_shapes=[
            pltpu.VMEM((heads, tq, LANES), F32),
            pltpu.VMEM((heads, tq, LANES), F32),
            pltpu.VMEM((heads, tq, FOX_DH), F32),
        ],
        compiler_params=_params("parallel", "parallel", "arbitrary"),
        name="fox_attn",
    )(fqkv, fqkv, fqkv, cq, ck)


def _dsa_kernel(q_ref, k_ref, v_ref, iq_ref, ik_ref, iw_ref, o_ref, sc_ref, m_ref, l_ref, acc_ref,
                *, tq, tk, top_k, max_iters):
    qi = pl.program_id(1)
    n_kc = (qi * tq + tq + tk - 1) // tk
    group = DSA_Q_HEADS // DSA_KV_HEADS
    qpos = qi * tq + lax.broadcasted_iota(jnp.int32, (tq, 1), 0)

    iq = jnp.concatenate([iq_ref[:, h * IDX_DH:(h + 1) * IDX_DH] for h in range(IDX_HEADS)], axis=0)
    iw = iw_ref[:, IDX_DH:IDX_DH + IDX_HEADS] * (IDX_HEADS ** -0.5 * IDX_DH ** -0.5)

    def score_chunk(c, carry):
        lo, hi = carry
        ks = pl.multiple_of(c * tk, tk)
        ik = ik_ref[pl.ds(ks, tk), 0:IDX_DH].astype(BF16)
        rel = _nt_dot(iq, ik)
        score = jnp.zeros((tq, tk), F32)
        for h in range(IDX_HEADS):
            score = score + iw[:, h:h + 1] * jnp.maximum(rel[h * tq:(h + 1) * tq, :], 0.0)
        kpos = ks + lax.broadcasted_iota(jnp.int32, (tq, tk), 1)
        valid = kpos <= qpos
        sc_ref[:, pl.ds(ks, tk)] = jnp.where(valid, score, NEG_INF)
        lo = jnp.minimum(lo, jnp.min(jnp.where(valid, score, jnp.inf), axis=-1, keepdims=True))
        hi = jnp.maximum(hi, jnp.max(jnp.where(valid, score, NEG_INF), axis=-1, keepdims=True))
        return lo, hi

    lo0, hi0 = lax.fori_loop(0, n_kc, score_chunk,
                             (jnp.full((tq, 1), jnp.inf, F32), jnp.full((tq, 1), NEG_INF, F32)))

    lane_reps = tk // LANES

    def count_ge(thr):
        thr_b = jnp.tile(jnp.broadcast_to(thr, (tq, LANES)), (1, lane_reps))

        def body(c, acc):
            ks = pl.multiple_of(c * tk, tk)
            hit = jnp.where(sc_ref[:, pl.ds(ks, tk)] >= thr_b, 1.0, 0.0)
            for r in range(lane_reps):
                acc = acc + hit[:, r * LANES:(r + 1) * LANES]
            return acc
        partial = lax.fori_loop(0, n_kc, body, jnp.zeros((tq, LANES), F32))
        return jnp.sum(partial, axis=-1, keepdims=True)

    k_f = float(top_k)
    n_keys = (qpos + 1).astype(F32)

    def unsettled(cnt_lo):
        return jnp.max(jnp.where(cnt_lo > k_f, 1.0, 0.0)) > 0.0

    def cond(state):
        it, _, _, cnt_lo = state
        return jnp.logical_and(it < max_iters, unsettled(cnt_lo))

    def refine(state):
        it, lo, hi, cnt_lo = state
        mid = 0.5 * (lo + hi)
        cnt = count_ge(mid)
        keep = cnt >= k_f
        lo = jnp.where(keep, mid, lo)
        cnt_lo = jnp.where(keep, cnt, cnt_lo)
        hi = jnp.where(keep, hi, mid)
        return it + 1, lo, hi, cnt_lo

    _, thr, _, _ = lax.while_loop(cond, refine, (jnp.int32(0), lo0, hi0, n_keys))

    m_ref[...] = jnp.full_like(m_ref, MASK_BIAS)
    l_ref[...] = jnp.zeros_like(l_ref)
    acc_ref[...] = jnp.zeros_like(acc_ref)
    c_exp = DSA_DH ** -0.5 * LOG2_E
    thr_b = jnp.tile(jnp.broadcast_to(thr, (tq, LANES)), (1, lane_reps))

    def attend(c, carry):
        ks = pl.multiple_of(c * tk, tk)
        bias = jnp.where(sc_ref[:, pl.ds(ks, tk)] >= thr_b, 0.0, MASK_BIAS)
        for g in range(DSA_KV_HEADS):
            heads = slice(g * group, (g + 1) * group)
            qg = jnp.concatenate(
                [q_ref[:, (g * group + j) * DSA_DH:(g * group + j + 1) * DSA_DH] for j in range(group)], axis=0)
            kc = k_ref[pl.ds(ks, tk), g * DSA_DH:(g + 1) * DSA_DH]
            vc = v_ref[pl.ds(ks, tk), g * DSA_DH:(g + 1) * DSA_DH]
            s = _nt_dot(qg, kc).reshape(group, tq, tk) + bias[None]
            m_prev = m_ref[heads]
            m_new = jnp.maximum(m_prev, jnp.max(s, axis=-1, keepdims=True))
            p = jnp.exp2((s - jnp.tile(m_new, (1, 1, lane_reps))) * c_exp)
            alpha = jnp.exp2((m_prev - m_new) * c_exp)
            p_sum = p[:, :, 0:LANES]
            for r in range(1, lane_reps):
                p_sum = p_sum + p[:, :, r * LANES:(r + 1) * LANES]
            l_ref[heads] = alpha * l_ref[heads] + p_sum
            pv = jnp.dot(p.reshape(group * tq, tk).astype(BF16), vc, preferred_element_type=F32)
            acc_ref[heads] = alpha * acc_ref[heads] + pv.reshape(group, tq, DSA_DH)
            m_ref[heads] = m_new
        return carry

    lax.fori_loop(0, n_kc, attend, 0)
    for h in range(DSA_Q_HEADS):
        l = jnp.sum(l_ref[h], axis=-1, keepdims=True)
        o_ref[:, h * DSA_DH:(h + 1) * DSA_DH] = (acc_ref[h] / l).astype(o_ref.dtype)


def _dsa(qk, v, iq, small, batch, seq_len, tq=256, tk=512):
    t = qk.shape[0]
    tk = min(tk, seq_len)
    nq = seq_len // tq
    qw = DSA_Q_HEADS * DSA_DH
    kw = DSA_KV_HEADS * DSA_DH
    top_k = min(DSA_TOPK_MAX, seq_len // 4)
    return pl.pallas_call(
        functools.partial(_dsa_kernel, tq=tq, tk=tk, top_k=top_k, max_iters=64),
        grid=(batch, nq),
        in_specs=[
            pl.BlockSpec((tq, qw), lambda b, i: (b * nq + i, 0)),
            pl.BlockSpec((seq_len, kw), lambda b, i: (b, qw // kw)),
            pl.BlockSpec((seq_len, kw), lambda b, i: (b, 0)),
            pl.BlockSpec((tq, IDX_HEADS * IDX_DH), lambda b, i: (b * nq + i, 0)),
            pl.BlockSpec((seq_len, LANES), lambda b, i: (b, 0)),
            pl.BlockSpec((tq, LANES), lambda b, i: (b * nq + i, 0)),
        ],
        out_specs=pl.BlockSpec((tq, qw), lambda b, i: (b * nq + i, 0)),
        out_shape=jax.ShapeDtypeStruct((t, qw), BF16),
        scratch_shapes=[
            pltpu.VMEM((tq, seq_len), F32),
            pltpu.VMEM((DSA_Q_HEADS, tq, LANES), F32),
            pltpu.VMEM((DSA_Q_HEADS, tq, LANES), F32),
            pltpu.VMEM((DSA_Q_HEADS, tq, DSA_DH), F32),
        ],
        compiler_params=_params("parallel", "arbitrary"),
        name="dsa",
    )(qk, qk, v, iq, small, small)


def _router_kernel(x_ref, w_ref, b_ref, o_ref):
    x = x_ref[...]
    w = w_ref[...]
    x_hi = x.astype(BF16)
    w_hi = w.astype(BF16)
    x_lo = (x - x_hi.astype(F32)).astype(BF16)
    w_lo = (w - w_hi.astype(F32)).astype(BF16)
    logits = (jnp.dot(x_hi, w_hi, preferred_element_type=F32)
              + (jnp.dot(x_hi, w_lo, preferred_element_type=F32) + jnp.dot(x_lo, w_hi, preferred_element_type=F32))
              + b_ref[...])
    lane = lax.broadcasted_iota(jnp.int32, logits.shape, 1)
    big = jnp.int32(LANES)
    g_logits = jnp.where(lane < N_GROUPS, logits, NEG_INF)
    g_max = jnp.max(g_logits, axis=-1, keepdims=True)
    g_prob = 1.0 / jnp.sum(jnp.exp(g_logits - g_max), axis=-1, keepdims=True)
    g_idx = jnp.min(jnp.where(g_logits == g_max, lane, big), axis=-1, keepdims=True)
    first = N_GROUPS + g_idx * EXPERTS_PER_GROUP
    e_logits = jnp.where((lane >= first) & (lane < first + EXPERTS_PER_GROUP), logits, NEG_INF)
    v1 = jnp.max(e_logits, axis=-1, keepdims=True)
    i1 = jnp.min(jnp.where(e_logits == v1, lane, big), axis=-1, keepdims=True)
    rest = jnp.where(lane == i1, NEG_INF, e_logits)
    v2 = jnp.max(rest, axis=-1, keepdims=True)
    i2 = jnp.min(jnp.where(rest == v2, lane, big), axis=-1, keepdims=True)
    e2 = jnp.exp(v2 - v1)
    w1 = (1.0 / (1.0 + e2)) * g_prob
    w2 = (e2 / (1.0 + e2)) * g_prob
    out = jnp.where(lane == 0, (i1 - N_GROUPS).astype(F32), 0.0)
    out = jnp.where(lane == 1, (i2 - N_GROUPS).astype(F32), out)
    out = jnp.where(lane == 2, w1, out)
    out = jnp.where(lane == 3, w2, out)
    o_ref[...] = out


def _router(x, w_cat, b_cat, tm=512):
    t, d = x.shape
    return pl.pallas_call(
        _router_kernel,
        grid=(t // tm,),
        in_specs=[pl.BlockSpec((tm, d), lambda i: (i, 0)),
                  pl.BlockSpec((d, LANES), lambda i: (0, 0)),
                  pl.BlockSpec((1, LANES), lambda i: (0, 0))],
        out_specs=pl.BlockSpec((tm, LANES), lambda i: (i, 0)),
        out_shape=jax.ShapeDtypeStruct((t, LANES), F32),
        compiler_params=_params("parallel"),
        name="moe_router",
    )(x, w_cat, b_cat)


def _plan_kernel(pos_ref, src_ref, dst_ref, *, n_tokens):
    def body(a, c):
        p = pos_ref[a]
        src_ref[p] = jnp.where(a >= n_tokens, a - n_tokens, a)
        dst_ref[p] = a
        return c

    lax.fori_loop(0, 2 * n_tokens, body, 0, unroll=8)


def _plan(pos, n_tokens):
    smem = pl.BlockSpec(memory_space=pltpu.SMEM)
    return pl.pallas_call(
        functools.partial(_plan_kernel, n_tokens=n_tokens),
        in_specs=[smem],
        out_specs=[smem, smem],
        out_shape=[jax.ShapeDtypeStruct((2 * n_tokens,), jnp.int32)] * 2,
        name="moe_plan",
    )(pos)


def _for_rows(n, tm, fn):
    def body(i, c):
        fn(i)
        return c

    @pl.when(n == tm)
    def _():
        for i in range(tm):
            fn(i)

    @pl.when(n < tm)
    def _():
        lax.fori_loop(0, n, body, 0)


def _experts_kernel(tile_e_ref, tile_start_ref, tile_n_ref, src_ref, dst_ref,
                    x_hbm, wg_ref, wu_ref, wd_ref, y_hbm, xbuf, ybuf, gsem, ssem, *, tm, n_tiles):
    del tile_e_ref
    t = pl.program_id(0)
    slot = t % 2

    def gather_row(tile, s, i):
        row = src_ref[tile_start_ref[tile] + i]
        return pltpu.make_async_copy(x_hbm.at[pl.ds(row, 1)], xbuf.at[s, pl.ds(i, 1)], gsem.at[s])

    def scatter_row(tile, s, i):
        row = dst_ref[tile_start_ref[tile] + i]
        return pltpu.make_async_copy(ybuf.at[s, pl.ds(i, 1)], y_hbm.at[pl.ds(row, 1)], ssem.at[s])

    def start_gather(tile, s):
        _for_rows(tile_n_ref[tile], tm, lambda i: gather_row(tile, s, i).start())

    def wait_rows(tile, bulk_copy, row_copy):
        n = tile_n_ref[tile]
        n8 = pl.multiple_of(n // SUBLANES * SUBLANES, SUBLANES)

        @pl.when(n8 > 0)
        def _():
            bulk_copy(n8).wait()

        def tail(i, c):
            row_copy(i).wait()
            return c
        lax.fori_loop(n8, n, tail, 0)

    def wait_gather(tile, s):
        wait_rows(tile,
                  lambda n8: pltpu.make_async_copy(x_hbm.at[pl.ds(0, n8)], xbuf.at[s, pl.ds(0, n8)], gsem.at[s]),
                  lambda i: gather_row(tile, s, i))

    def wait_scatter(tile, s):
        wait_rows(tile,
                  lambda n8: pltpu.make_async_copy(ybuf.at[s, pl.ds(0, n8)], y_hbm.at[pl.ds(0, n8)], ssem.at[s]),
                  lambda i: scatter_row(tile, s, i))

    @pl.when(t == 0)
    def _():
        xbuf[...] = jnp.zeros_like(xbuf)
        start_gather(0, 0)

    @pl.when(t + 1 < n_tiles)
    def _():
        start_gather(t + 1, 1 - slot)

    @pl.when(t >= 2)
    def _():
        wait_scatter(t - 2, slot)

    n_rows = tile_n_ref[t]

    @pl.when(n_rows > 0)
    def _():
        wait_gather(t, slot)
        xb = xbuf[slot].astype(BF16)
        gate = jnp.dot(xb, wg_ref[0, 0, 0].astype(BF16), preferred_element_type=F32)
        up = jnp.dot(xb, wu_ref[0, 0, 0].astype(BF16), preferred_element_type=F32)
        hidden = (_silu(gate) * up).astype(BF16)
        ybuf[slot] = jnp.dot(hidden, wd_ref[0, 0, 0].astype(BF16), preferred_element_type=F32)
        _for_rows(n_rows, tm, lambda i: scatter_row(t, slot, i).start())

    @pl.when(t == n_tiles - 1)
    def _():
        if n_tiles >= 2:
            wait_scatter(t - 1, 1 - slot)
        wait_scatter(t, slot)


def _experts(x, w_gate, w_up, w_down, layer, tile_e, tile_start, tile_n, src_rows, dst_rows, tm, n_tiles):
    t_tokens, d = x.shape
    ff = w_gate.shape[-1]
    w_map = lambda t, te, ts, tn, sr, ds_: (layer, te[t] // EXPERTS_PER_GROUP, te[t] % EXPERTS_PER_GROUP, 0, 0)
    grid_spec = pltpu.PrefetchScalarGridSpec(
        num_scalar_prefetch=5,
        grid=(n_tiles,),
        in_specs=[
            pl.BlockSpec(memory_space=pl.ANY),
            pl.BlockSpec((1, 1, 1, d, ff), w_map),
            pl.BlockSpec((1, 1, 1, d, ff), w_map),
            pl.BlockSpec((1, 1, 1, ff, d), w_map),
        ],
        out_specs=pl.BlockSpec(memory_space=pl.ANY),
        scratch_shapes=[
            pltpu.VMEM((2, tm, d), F32),
            pltpu.VMEM((2, tm, d), F32),
            pltpu.SemaphoreType.DMA((2,)),
            pltpu.SemaphoreType.DMA((2,)),
        ],
    )
    return pl.pallas_call(
        functools.partial(_experts_kernel, tm=tm, n_tiles=n_tiles),
        grid_spec=grid_spec,
        out_shape=jax.ShapeDtypeStruct((2 * t_tokens, d), F32),
        compiler_params=_params("arbitrary"),
        name="moe_experts",
    )(tile_e, tile_start, tile_n, src_rows, dst_rows, x, w_gate, w_up, w_down)


def _combine_ln_kernel(x_ref, y0_ref, y1_ref, r_ref, g_ref, b_ref, o_ref, o_bf_ref):
    w0 = r_ref[:, 2:3]
    w1 = r_ref[:, 3:4]
    y = DN_ALPHA * x_ref[...] + (w0 * y0_ref[...] + w1 * y1_ref[...])
    out = _layer_norm(y, g_ref[...], b_ref[...])
    o_ref[...] = out
    o_bf_ref[...] = out.astype(BF16)


def _combine_ln(x, y2, route, g, b, tm=512):
    t, d = x.shape
    n_t = t // tm
    return pl.pallas_call(
        _combine_ln_kernel,
        grid=(n_t,),
        in_specs=[
            pl.BlockSpec((tm, d), lambda i: (i, 0)),
            pl.BlockSpec((tm, d), lambda i: (i, 0)),
            pl.BlockSpec((tm, d), lambda i: (i + n_t, 0)),
            pl.BlockSpec((tm, LANES), lambda i: (i, 0)),
            pl.BlockSpec((1, d), lambda i: (0, 0)),
            pl.BlockSpec((1, d), lambda i: (0, 0)),
        ],
        out_specs=[pl.BlockSpec((tm, d), lambda i: (i, 0)), pl.BlockSpec((tm, d), lambda i: (i, 0))],
        out_shape=[jax.ShapeDtypeStruct((t, d), F32), jax.ShapeDtypeStruct((t, d), BF16)],
        compiler_params=_params("parallel"),
        name="moe_combine_ln",
    )(x, y2, y2, route, g.reshape(1, d), b.reshape(1, d))


def _moe_layer(x, wg, bg, we, be, w_gate, w_up, w_down, layer, ln_g, ln_b, tm=256):
    t, d = x.shape
    w_cat = jnp.concatenate([wg, jnp.transpose(we, (1, 0, 2)).reshape(d, N_EXPERTS)], axis=1)
    w_cat = jnp.pad(w_cat, ((0, 0), (0, LANES - w_cat.shape[1])))
    b_cat = jnp.pad(jnp.concatenate([bg, be.reshape(-1)]), (0, LANES - N_GROUPS - N_EXPERTS)).reshape(1, LANES)
    route = _router(x, w_cat, b_cat)

    i32 = jnp.int32
    expert = jnp.concatenate([route[:, 0], route[:, 1]]).astype(i32)
    e_ids = jnp.arange(N_EXPERTS, dtype=i32)
    onehot = (expert[:, None] == e_ids[None, :]).astype(i32)
    running = jnp.cumsum(onehot, axis=0)
    counts = running[-1]
    seg_start = jnp.cumsum(counts) - counts
    pos = jnp.sum((running - onehot + seg_start[None, :]) * onehot, axis=1)
    tiles_per = (counts + tm - 1) // tm
    tile_end = jnp.cumsum(tiles_per)
    tile_begin = tile_end - tiles_per
    n_tiles = (2 * t) // tm + N_EXPERTS
    tile_ids = jnp.arange(n_tiles, dtype=i32)
    n_used = tile_end[-1]
    tile_e = jnp.sum((tile_end[None, :] <= jnp.minimum(tile_ids, n_used - 1)[:, None]).astype(i32), axis=1)
    tile_e = jnp.clip(tile_e, 0, N_EXPERTS - 1)
    te_onehot = (tile_e[:, None] == e_ids[None, :]).astype(i32)
    within = (tile_ids - jnp.sum(te_onehot * tile_begin[None, :], axis=1)) * tm
    tile_n = jnp.clip(jnp.sum(te_onehot * counts[None, :], axis=1) - within, 0, tm)
    tile_n = jnp.where(tile_ids < n_used, tile_n, 0).astype(i32)
    tile_start = jnp.where(tile_n > 0, jnp.sum(te_onehot * seg_start[None, :], axis=1) + within, 0).astype(i32)
    src_rows, dst_rows = _plan(pos.astype(i32), t)
    y2 = _experts(x, w_gate, w_up, w_down, layer, tile_e.astype(i32), tile_start, tile_n, src_rows, dst_rows,
                  tm, n_tiles)
    return _combine_ln(x, y2, route, ln_g, ln_b)


def _even_mixer_layer(x, xb, batch, seq_len, w_in, gate_w2, gate_b, norm_g, fox_gate_b, w_out, ln_g, ln_b):
    gla_w = 2 * GLA_HEADS * GLA_DK + 2 * GLA_HEADS * GLA_DV
    fox_w = 3 * FOX_HEADS * FOX_DH
    glr0 = gla_w
    fox0 = glr0 + GLA_GATE_RANK
    ff0 = fox0 + fox_w
    main = _project(xb, w_in, 0, gla_w, F32, seq_len)
    fqkv = _project(xb, w_in[:, fox0:ff0], 0, fox_w, BF16, seq_len)
    w_small = jnp.concatenate([w_in[:, ff0:ff0 + FOX_HEADS], w_in[:, glr0:fox0]], axis=1)
    w_small = jnp.pad(w_small, ((0, 0), (0, LANES - w_small.shape[1])))
    small = _project(xb, w_small, 0, LANES, F32, seq_len)
    o_gla = _gla(main, small, gate_w2, gate_b, norm_g, batch, seq_len)
    gate_b_row = jnp.pad(fox_gate_b, (0, LANES - FOX_HEADS)).reshape(1, LANES)
    c, c_t = _fox_gate(small, gate_b_row, batch, seq_len)
    fox_heads_per_step = 4
    cq = c.reshape(batch, seq_len, LANES)[:, :, :FOX_HEADS]
    cq = jnp.transpose(cq.reshape(batch, seq_len, FOX_HEADS // fox_heads_per_step, fox_heads_per_step), (0, 2, 1, 3))
    ck = c_t[:, :FOX_HEADS, :].reshape(batch * FOX_HEADS, 1, seq_len)
    o_fox = _fox(fqkv, cq, ck, batch, seq_len, heads=fox_heads_per_step)
    return _out_proj_ln(o_gla, o_fox, 0, _to_bf16(w_out), x, ln_g, ln_b)


def _odd_mixer_layer(x, xb, batch, seq_len, w_in, w_out, ln_g, ln_b):
    qw = DSA_Q_HEADS * DSA_DH
    kw = DSA_KV_HEADS * DSA_DH
    iqw = IDX_HEADS * IDX_DH
    rope_attn = (DSA_DH,) + _rope_tables(seq_len, DSA_DH)
    rope_idx = (IDX_DH,) + _rope_tables(seq_len, IDX_DH)
    qk = _project(xb, w_in, 0, qw + kw, BF16, seq_len, rope=rope_attn)
    v = _project(xb, w_in, qw + kw, kw, BF16, seq_len)
    iq = _project(xb, w_in, qw + 2 * kw, iqw, BF16, seq_len, rope=rope_idx)
    small0 = qw + 2 * kw + iqw
    w_small = jnp.pad(w_in[:, small0:], ((0, 0), (0, LANES - (w_in.shape[1] - small0))))
    small = _project(xb, w_small, 0, LANES, F32, seq_len, rope=rope_idx, rope_lanes=IDX_DH)
    o = _dsa(qk, v, iq, small, batch, seq_len)
    return _out_proj_ln(o, o, 1, _to_bf16(w_out), x, ln_g, ln_b)


def kernel(x, a_w_in, a_gla_gate_w2, a_gla_gate_b, a_gla_norm_g, a_fox_gate_b, a_w_out, c_w_in, c_w_out,
           ln_mix_g, ln_mix_b, ln_ffn_g, ln_ffn_b, moe_group_w, moe_group_b, moe_expert_w, moe_expert_b,
           moe_w_gate, moe_w_up, moe_w_down):
    batch, seq_len, d = x.shape
    h = x.reshape(batch * seq_len, d)
    hb = _to_bf16(h)
    for layer in range(DEPTH):
        j = layer // 2
        if layer % 2 == 0:
            h, hb = _even_mixer_layer(h, hb, batch, seq_len, a_w_in[j], a_gla_gate_w2[j], a_gla_gate_b[j],
                                      a_gla_norm_g[j], a_fox_gate_b[j], a_w_out[j],
                                      ln_mix_g[layer], ln_mix_b[layer])
        else:
            h, hb = _odd_mixer_layer(h, hb, batch, seq_len, c_w_in[j], c_w_out[j],
                                     ln_mix_g[layer], ln_mix_b[layer])
        h, hb = _moe_layer(h, moe_group_w[layer], moe_group_b[layer], moe_expert_w[layer], moe_expert_b[layer],
                           moe_w_gate, moe_w_up, moe_w_down, layer, ln_ffn_g[layer], ln_ffn_b[layer])
    return h.reshape(batch, seq_len, d)
```

```python
import functools

import jax
import jax.numpy as jnp
import numpy as np
from jax import lax
from jax.experimental import pallas as pl
from jax.experimental.pallas import tpu as pltpu

F32 = jnp.float32
BF16 = jnp.bfloat16

DEPTH = 2
GLA_HEADS = 4
GLA_DK = 128
GLA_DV = 256
GLA_GATE_RANK = 16
GLA_GATE_TAU = 16.0
GLA_CHUNK = 64
FOX_HEADS = 8
FOX_DH = 128
DSA_Q_HEADS = 16
DSA_KV_HEADS = 4
DSA_DH = 128
IDX_HEADS = 16
IDX_DH = 64
DSA_TOPK_MAX = 256
N_GROUPS = 4
EXPERTS_PER_GROUP = 8
N_EXPERTS = N_GROUPS * EXPERTS_PER_GROUP
ROPE_THETA = 10000.0
LN_EPS = 1e-5
DN_ALPHA = (2 * DEPTH) ** 0.25

LANES = 128
SUBLANES = 8
VMEM_LIMIT_BYTES = 56 * 1024 * 1024

NEG_INF = float("-inf")
LOG2_E = 1.4426950408889634
MASK_BIAS = -1e30


def _params(*semantics):
    return pltpu.CompilerParams(dimension_semantics=semantics, vmem_limit_bytes=VMEM_LIMIT_BYTES)


def _nt_dot(a, b):
    return lax.dot_general(a, b, (((1,), (1,)), ((), ())), preferred_element_type=F32)


def _tn_dot(a, b):
    return lax.dot_general(a, b, (((0,), (0,)), ((), ())), preferred_element_type=F32)


def _log_sigmoid(z):
    return jnp.minimum(z, 0.0) - jnp.log1p(jnp.exp(-jnp.abs(z)))


def _silu(z):
    return z * (1.0 / (1.0 + jnp.exp(-z)))


def _cumsum_rows(x):
    n = x.shape[0]
    row = lax.broadcasted_iota(jnp.int32, x.shape, 0)
    shift = 1
    while shift < n:
        x = x + jnp.where(row >= shift, pltpu.roll(x, shift, 0), 0.0)
        shift *= 2
    return x


def _rot_half(x, head_dim):
    half = head_dim // 2
    n = x.shape[-1]
    lane = lax.broadcasted_iota(jnp.int32, x.shape, 1)
    fwd = pltpu.roll(x, n - half, 1)
    bwd = pltpu.roll(x, half, 1)
    return jnp.where((lane % head_dim) < half, fwd, bwd)


def _layer_norm(y, g, b):
    mu = jnp.mean(y, axis=-1, keepdims=True)
    d = y - mu
    var = jnp.mean(d * d, axis=-1, keepdims=True)
    return d * lax.rsqrt(var + LN_EPS) * g + b


def _cast_kernel(x_ref, o_ref):
    o_ref[...] = x_ref[...].astype(o_ref.dtype)


def _to_bf16(a, rows=512):
    r, c = a.shape
    rows = min(rows, r)
    return pl.pallas_call(
        _cast_kernel,
        grid=(r // rows,),
        in_specs=[pl.BlockSpec((rows, c), lambda i: (i, 0))],
        out_specs=pl.BlockSpec((rows, c), lambda i: (i, 0)),
        out_shape=jax.ShapeDtypeStruct((r, c), BF16),
        compiler_params=_params("parallel"),
        name="to_bf16",
    )(a)


def _proj_kernel(*refs, rope_dim, rope_lanes):
    if rope_dim:
        x_ref, w_ref, cos_ref, sin_ref, o_ref = refs
    else:
        x_ref, w_ref, o_ref = refs
    acc = jnp.dot(x_ref[...], w_ref[...].astype(BF16), preferred_element_type=F32)
    if rope_dim:
        reps = acc.shape[-1] // LANES
        cos = jnp.tile(cos_ref[...], (1, reps))
        sin = jnp.tile(sin_ref[...], (1, reps))
        roped = acc * cos + _rot_half(acc, rope_dim) * sin
        if rope_lanes < acc.shape[-1]:
            lane = lax.broadcasted_iota(jnp.int32, acc.shape, 1)
            roped = jnp.where(lane < rope_lanes, roped, acc)
        acc = roped
    o_ref[...] = acc.astype(o_ref.dtype)


def _project(x, w, col_start, n_cols, out_dtype, seq_len, rope=None, rope_lanes=None, tm=1024, tn=512):
    t, k = x.shape
    tm = min(tm, seq_len)
    tn = min(tn, n_cols)
    assert t % tm == 0 and n_cols % tn == 0 and col_start % tn == 0 and seq_len % tm == 0
    col_off = col_start // tn
    in_specs = [
        pl.BlockSpec((tm, k), lambda i, j: (i, 0)),
        pl.BlockSpec((k, tn), lambda i, j: (0, j + col_off)),
    ]
    args = [x, w]
    rope_dim = 0
    if rope is not None:
        rope_dim, cos, sin = rope
        seq_tiles = seq_len // tm
        in_specs += [pl.BlockSpec((tm, LANES), lambda i, j: (i % seq_tiles, 0))] * 2
        args += [cos, sin]
    return pl.pallas_call(
        functools.partial(_proj_kernel, rope_dim=rope_dim, rope_lanes=rope_lanes or n_cols),
        grid=(t // tm, n_cols // tn),
        in_specs=in_specs,
        out_specs=pl.BlockSpec((tm, tn), lambda i, j: (i, j)),
        out_shape=jax.ShapeDtypeStruct((t, n_cols), out_dtype),
        compiler_params=_params("parallel", "parallel"),
        name="in_proj",
    )(*args)


def _rope_tables(seq_len, head_dim):
    half = head_dim // 2
    inv = ROPE_THETA ** (-jnp.arange(half, dtype=F32) / half)
    ang = jnp.arange(seq_len, dtype=F32)[:, None] * inv[None, :]
    cos, sin = jnp.cos(ang), jnp.sin(ang)
    reps = LANES // head_dim
    cos_full = jnp.tile(jnp.concatenate([cos, cos], axis=-1), (1, reps))
    sin_signed = jnp.tile(jnp.concatenate([-sin, sin], axis=-1), (1, reps))
    return cos_full, sin_signed


def _out_proj_ln_kernel(oa_ref, ob_ref, w_ref, x_ref, g_ref, b_ref, out_ref, out_bf_ref):
    half = oa_ref.shape[1]
    h = jnp.dot(oa_ref[...], w_ref[0:half, :], preferred_element_type=F32)
    h = h + jnp.dot(ob_ref[...], w_ref[half:2 * half, :], preferred_element_type=F32)
    out = _layer_norm(DN_ALPHA * x_ref[...] + h, g_ref[...], b_ref[...])
    out_ref[...] = out
    out_bf_ref[...] = out.astype(BF16)


def _out_proj_ln(o_a, o_b, b_col_block, w_bf16, x, g, b, tm=512):
    t = o_a.shape[0]
    kdim, d = w_bf16.shape
    half = kdim // 2
    tm = min(tm, t)
    return pl.pallas_call(
        _out_proj_ln_kernel,
        grid=(t // tm,),
        in_specs=[
            pl.BlockSpec((tm, half), lambda i: (i, 0)),
            pl.BlockSpec((tm, half), lambda i: (i, b_col_block)),
            pl.BlockSpec((kdim, d), lambda i: (0, 0)),
            pl.BlockSpec((tm, d), lambda i: (i, 0)),
            pl.BlockSpec((1, d), lambda i: (0, 0)),
            pl.BlockSpec((1, d), lambda i: (0, 0)),
        ],
        out_specs=[pl.BlockSpec((tm, d), lambda i: (i, 0)), pl.BlockSpec((tm, d), lambda i: (i, 0))],
        out_shape=[jax.ShapeDtypeStruct((t, d), F32), jax.ShapeDtypeStruct((t, d), BF16)],
        compiler_params=_params("parallel"),
        name="out_proj_ln",
    )(o_a, o_b, w_bf16, x, g.reshape(1, d), b.reshape(1, d))


def _gla_kernel(q_ref, k_ref, v_ref, r_ref, sm_ref, w2_ref, gb_ref, ng_ref, o_ref, state_ref, *, n_chunks):
    c_len, dk, dv = GLA_CHUNK, GLA_DK, GLA_DV

    @pl.when(pl.program_id(1) == 0)
    def _():
        state_ref[...] = jnp.zeros_like(state_ref)

    row = lax.broadcasted_iota(jnp.int32, (c_len, c_len), 0)
    col = lax.broadcasted_iota(jnp.int32, (c_len, c_len), 1)
    causal = row >= col
    norm_g = ng_ref[...]
    for c in range(n_chunks):
        rows = slice(c * c_len, (c + 1) * c_len)
        glr = sm_ref[rows, FOX_HEADS:FOX_HEADS + GLA_GATE_RANK].astype(BF16)
        for h in range(GLA_HEADS):
            kcols = slice(h * dk, (h + 1) * dk)
            vcols = slice(h * dv, (h + 1) * dv)
            z = jnp.dot(glr, w2_ref[:, kcols].astype(BF16), preferred_element_type=F32) + gb_ref[:, kcols]
            log_a = _log_sigmoid(z) / GLA_GATE_TAU
            b = _cumsum_rows(log_a)
            b_last = b[c_len - 1:c_len, :]
            q = q_ref[rows, kcols] * (dk ** -0.5)
            k = k_ref[rows, kcols]
            v = v_ref[rows, vcols].astype(BF16)
            q_dec = (q * jnp.exp(b)).astype(BF16)
            k_inv = (k * jnp.exp(-b)).astype(BF16)
            k_end = (k * jnp.exp(b_last - b)).astype(BF16)
            scores = jnp.where(causal, _nt_dot(q_dec, k_inv), 0.0)
            state_t = state_ref[h]
            o = jnp.dot(scores.astype(BF16), v, preferred_element_type=F32)
            o = o + _nt_dot(q_dec, state_t.astype(BF16))
            state_ref[h] = jnp.exp(b_last) * state_t + _tn_dot(v, k_end)
            rms = lax.rsqrt(jnp.mean(o * o, axis=-1, keepdims=True) + LN_EPS)
            o_ref[rows, vcols] = (o * rms * norm_g * _silu(r_ref[rows, vcols])).astype(o_ref.dtype)


def _gla(main, small, gate_w2, gate_b, norm_g, batch, seq_len, rows_per_step=256):
    t = main.shape[0]
    n_steps = seq_len // rows_per_step
    hk, hv = GLA_HEADS * GLA_DK, GLA_HEADS * GLA_DV
    row_map = lambda b, n: b * n_steps + n
    return pl.pallas_call(
        functools.partial(_gla_kernel, n_chunks=rows_per_step // GLA_CHUNK),
        grid=(batch, n_steps),
        in_specs=[
            pl.BlockSpec((rows_per_step, hk), lambda b, n: (row_map(b, n), 0)),
            pl.BlockSpec((rows_per_step, hk), lambda b, n: (row_map(b, n), 1)),
            pl.BlockSpec((rows_per_step, hv), lambda b, n: (row_map(b, n), 1)),
            pl.BlockSpec((rows_per_step, hv), lambda b, n: (row_map(b, n), 2)),
            pl.BlockSpec((rows_per_step, LANES), lambda b, n: (row_map(b, n), 0)),
            pl.BlockSpec((GLA_GATE_RANK, hk), lambda b, n: (0, 0)),
            pl.BlockSpec((1, hk), lambda b, n: (0, 0)),
            pl.BlockSpec((1, GLA_DV), lambda b, n: (0, 0)),
        ],
        out_specs=pl.BlockSpec((rows_per_step, hv), lambda b, n: (row_map(b, n), 0)),
        out_shape=jax.ShapeDtypeStruct((t, hv), BF16),
        scratch_shapes=[pltpu.VMEM((GLA_HEADS, GLA_DV, GLA_DK), F32)],
        compiler_params=_params("parallel", "arbitrary"),
        name="gla",
    )(main, main, main, main, small, gate_w2, gate_b.reshape(1, hk), norm_g.reshape(1, GLA_DV))


def _fox_gate_kernel(sm_ref, b_ref, c_ref, ct_ref):
    c = _cumsum_rows(_log_sigmoid(sm_ref[...] + b_ref[...]))
    c_ref[...] = c
    ct_ref[0] = c.T


def _fox_gate(small, gate_b_row, batch, seq_len):
    t = small.shape[0]
    return pl.pallas_call(
        _fox_gate_kernel,
        grid=(batch,),
        in_specs=[pl.BlockSpec((seq_len, LANES), lambda b: (b, 0)), pl.BlockSpec((1, LANES), lambda b: (0, 0))],
        out_specs=[pl.BlockSpec((seq_len, LANES), lambda b: (b, 0)),
                   pl.BlockSpec((1, LANES, seq_len), lambda b: (b, 0, 0))],
        out_shape=[jax.ShapeDtypeStruct((t, LANES), F32), jax.ShapeDtypeStruct((batch, LANES, seq_len), F32)],
        compiler_params=_params("parallel"),
        name="fox_gate",
    )(small, gate_b_row)


def _fox_kernel(q_ref, k_ref, v_ref, cq_ref, ck_ref, o_ref, m_ref, l_ref, acc_ref, *, tq, heads):
    qi = pl.program_id(2)
    dh = FOX_DH
    scale = dh ** -0.5
    inv_scale = dh ** 0.5
    c_exp = scale * LOG2_E
    reps = tq // LANES

    m_ref[...] = jnp.full_like(m_ref, MASK_BIAS)
    l_ref[...] = jnp.zeros_like(l_ref)
    acc_ref[...] = jnp.zeros_like(acc_ref)

    def step(j, masked):
        ks = pl.multiple_of(j * tq, tq)
        for h in range(heads):
            cols = slice(h * dh, (h + 1) * dh)
            q = q_ref[:, cols]
            k = k_ref[pl.ds(ks, tq), cols]
            v = v_ref[pl.ds(ks, tq), cols]
            cq = cq_ref[0, 0, :, h:h + 1] * inv_scale
            s = _nt_dot(q, k) - ck_ref[h, :, pl.ds(ks, tq)] * inv_scale
            if masked:
                row = lax.broadcasted_iota(jnp.int32, s.shape, 0)
                col = lax.broadcasted_iota(jnp.int32, s.shape, 1)
                s = jnp.where(col <= row, s, MASK_BIAS)
            m_prev = m_ref[h]
            m_new = jnp.maximum(m_prev, jnp.max(s, axis=-1, keepdims=True) + cq)
            shift = m_new - cq
            p = jnp.exp2((s - jnp.tile(shift, (1, reps))) * c_exp)
            alpha = jnp.exp2((m_prev - m_new) * c_exp)
            p_sum = p[:, 0:LANES]
            for r in range(1, reps):
                p_sum = p_sum + p[:, r * LANES:(r + 1) * LANES]
            l_ref[h] = alpha * l_ref[h] + p_sum
            acc_ref[h] = alpha * acc_ref[h] + jnp.dot(p.astype(BF16), v, preferred_element_type=F32)
            m_ref[h] = m_new

    def body(j, carry):
        step(j, False)
        return carry

    lax.fori_loop(0, qi, body, 0)
    step(qi, True)
    for h in range(heads):
        l = jnp.sum(l_ref[h], axis=-1, keepdims=True)
        o_ref[:, h * dh:(h + 1) * dh] = (acc_ref[h] / l).astype(o_ref.dtype)


def _fox(fqkv, cq, ck, batch, seq_len, tq=1024, heads=4):
    t = fqkv.shape[0]
    nq = seq_len // tq
    n_hg = FOX_HEADS // heads
    w = heads * FOX_DH
    return pl.pallas_call(
        functools.partial(_fox_kernel, tq=tq, heads=heads),
        grid=(batch, n_hg, nq),
        in_specs=[
            pl.BlockSpec((tq, w), lambda b, g, i: (b * nq + i, g)),
            pl.BlockSpec((seq_len, w), lambda b, g, i: (b, n_hg + g)),
            pl.BlockSpec((seq_len, w), lambda b, g, i: (b, 2 * n_hg + g)),
            pl.BlockSpec((1, 1, tq, heads), lambda b, g, i: (b, g, i, 0)),
            pl.BlockSpec((heads, 1, seq_len), lambda b, g, i: (b * n_hg + g, 0, 0)),
        ],
        out_specs=pl.BlockSpec((tq, w), lambda b, g, i: (b * nq + i, g)),
        out_shape=jax.ShapeDtypeStruct((t, FOX_HEADS * FOX_DH), BF16),
        scratch_shapes=[
            pltpu.VMEM((heads, tq, LANES), F32),
            pltpu.VMEM((heads, tq, LANES), F32),
            pltpu.VMEM((heads, tq, FOX_DH), F32),
        ],
        compiler_params=_params("parallel", "parallel", "arbitrary"),
        name="fox_attn",
    )(fqkv, fqkv, fqkv, cq, ck)


def _dsa_kernel(q_ref, k_ref, v_ref, iq_ref, ik_ref, iw_ref, o_ref, sc_ref, m_ref, l_ref, acc_ref,
                *, tq, tk, top_k, max_iters):
    qi = pl.program_id(1)
    n_kc = (qi * tq + tq + tk - 1) // tk
    group = DSA_Q_HEADS // DSA_KV_HEADS
    qpos = qi * tq + lax.broadcasted_iota(jnp.int32, (tq, 1), 0)

    iq = jnp.concatenate([iq_ref[:, h * IDX_DH:(h + 1) * IDX_DH] for h in range(IDX_HEADS)], axis=0)
    iw = iw_ref[:, IDX_DH:IDX_DH + IDX_HEADS] * (IDX_HEADS ** -0.5 * IDX_DH ** -0.5)

    def score_chunk(c, carry):
        lo, hi = carry
        ks = pl.multiple_of(c * tk, tk)
        ik = ik_ref[pl.ds(ks, tk), 0:IDX_DH].astype(BF16)
        rel = _nt_dot(iq, ik)
        score = jnp.zeros((tq, tk), F32)
        for h in range(IDX_HEADS):
            score = score + iw[:, h:h + 1] * jnp.maximum(rel[h * tq:(h + 1) * tq, :], 0.0)
        kpos = ks + lax.broadcasted_iota(jnp.int32, (tq, tk), 1)
        valid = kpos <= qpos
        sc_ref[:, pl.ds(ks, tk)] = jnp.where(valid, score, NEG_INF)
        lo = jnp.minimum(lo, jnp.min(jnp.where(valid, score, jnp.inf), axis=-1, keepdims=True))
        hi = jnp.maximum(hi, jnp.max(jnp.where(valid, score, NEG_INF), axis=-1, keepdims=True))
        return lo, hi

    lo0, hi0 = lax.fori_loop(0, n_kc, score_chunk,
                             (jnp.full((tq, 1), jnp.inf, F32), jnp.full((tq, 1), NEG_INF, F32)))

    lane_reps = tk // LANES

    def count_ge(thr):
        thr_b = jnp.tile(jnp.broadcast_to(thr, (tq, LANES)), (1, lane_reps))

        def body(c, acc):
            ks = pl.multiple_of(c * tk, tk)
            hit = jnp.where(sc_ref[:, pl.ds(ks, tk)] >= thr_b, 1.0, 0.0)
            for r in range(lane_reps):
                acc = acc + hit[:, r * LANES:(r + 1) * LANES]
            return acc
        partial = lax.fori_loop(0, n_kc, body, jnp.zeros((tq, LANES), F32))
        return jnp.sum(partial, axis=-1, keepdims=True)

    k_f = float(top_k)
    n_keys = (qpos + 1).astype(F32)

    def unsettled(cnt_lo):
        return jnp.max(jnp.where(cnt_lo > k_f, 1.0, 0.0)) > 0.0

    def cond(state):
        it, _, _, cnt_lo = state
        return jnp.logical_and(it < max_iters, unsettled(cnt_lo))

    def refine(state):
        it, lo, hi, cnt_lo = state
        mid = 0.5 * (lo + hi)
        cnt = count_ge(mid)
        keep = cnt >= k_f
        lo = jnp.where(keep, mid, lo)
        cnt_lo = jnp.where(keep, cnt, cnt_lo)
        hi = jnp.where(keep, hi, mid)
        return it + 1, lo, hi, cnt_lo

    _, thr, _, _ = lax.while_loop(cond, refine, (jnp.int32(0), lo0, hi0, n_keys))

    m_ref[...] = jnp.full_like(m_ref, MASK_BIAS)
    l_ref[...] = jnp.zeros_like(l_ref)
    acc_ref[...] = jnp.zeros_like(acc_ref)
    c_exp = DSA_DH ** -0.5 * LOG2_E
    thr_b = jnp.tile(jnp.broadcast_to(thr, (tq, LANES)), (1, lane_reps))

    def attend(c, carry):
        ks = pl.multiple_of(c * tk, tk)
        bias = jnp.where(sc_ref[:, pl.ds(ks, tk)] >= thr_b, 0.0, MASK_BIAS)
        for g in range(DSA_KV_HEADS):
            heads = slice(g * group, (g + 1) * group)
            qg = jnp.concatenate(
                [q_ref[:, (g * group + j) * DSA_DH:(g * group + j + 1) * DSA_DH] for j in range(group)], axis=0)
            kc = k_ref[pl.ds(ks, tk), g * DSA_DH:(g + 1) * DSA_DH]
            vc = v_ref[pl.ds(ks, tk), g * DSA_DH:(g + 1) * DSA_DH]
            s = _nt_dot(qg, kc).reshape(group, tq, tk) + bias[None]
            m_prev = m_ref[heads]
            m_new = jnp.maximum(m_prev, jnp.max(s, axis=-1, keepdims=True))
            p = jnp.exp2((s - jnp.tile(m_new, (1, 1, lane_reps))) * c_exp)
            alpha = jnp.exp2((m_prev - m_new) * c_exp)
            p_sum = p[:, :, 0:LANES]
            for r in range(1, lane_reps):
                p_sum = p_sum + p[:, :, r * LANES:(r + 1) * LANES]
            l_ref[heads] = alpha * l_ref[heads] + p_sum
            pv = jnp.dot(p.reshape(group * tq, tk).astype(BF16), vc, preferred_element_type=F32)
            acc_ref[heads] = alpha * acc_ref[heads] + pv.reshape(group, tq, DSA_DH)
            m_ref[heads] = m_new
        return carry

    lax.fori_loop(0, n_kc, attend, 0)
    for h in range(DSA_Q_HEADS):
        l = jnp.sum(l_ref[h], axis=-1, keepdims=True)
        o_ref[:, h * DSA_DH:(h + 1) * DSA_DH] = (acc_ref[h] / l).astype(o_ref.dtype)


def _dsa(qk, v, iq, small, batch, seq_len, tq=256, tk=512):
    t = qk.shape[0]
    tk = min(tk, seq_len)
    nq = seq_len // tq
    qw = DSA_Q_HEADS * DSA_DH
    kw = DSA_KV_HEADS * DSA_DH
    top_k = min(DSA_TOPK_MAX, seq_len // 4)
    return pl.pallas_call(
        functools.partial(_dsa_kernel, tq=tq, tk=tk, top_k=top_k, max_iters=64),
        grid=(batch, nq),
        in_specs=[
            pl.BlockSpec((tq, qw), lambda b, i: (b * nq + i, 0)),
            pl.BlockSpec((seq_len, kw), lambda b, i: (b, qw // kw)),
            pl.BlockSpec((seq_len, kw), lambda b, i: (b, 0)),
            pl.BlockSpec((tq, IDX_HEADS * IDX_DH), lambda b, i: (b * nq + i, 0)),
            pl.BlockSpec((seq_len, LANES), lambda b, i: (b, 0)),
            pl.BlockSpec((tq, LANES), lambda b, i: (b * nq + i, 0)),
        ],
        out_specs=pl.BlockSpec((tq, qw), lambda b, i: (b * nq + i, 0)),
        out_shape=jax.ShapeDtypeStruct((t, qw), BF16),
        scratch_shapes=[
            pltpu.VMEM((tq, seq_len), F32),
            pltpu.VMEM((DSA_Q_HEADS, tq, LANES), F32),
            pltpu.VMEM((DSA_Q_HEADS, tq, LANES), F32),
            pltpu.VMEM((DSA_Q_HEADS, tq, DSA_DH), F32),
        ],
        compiler_params=_params("parallel", "arbitrary"),
        name="dsa",
    )(qk, qk, v, iq, small, small)


def _router_kernel(x_ref, w_ref, b_ref, o_ref, count_ref):
    x = x_ref[...]
    w = w_ref[...]
    x_hi = x.astype(BF16)
    w_hi = w.astype(BF16)
    x_lo = (x - x_hi.astype(F32)).astype(BF16)
    w_lo = (w - w_hi.astype(F32)).astype(BF16)
    logits = (jnp.dot(x_hi, w_hi, preferred_element_type=F32)
              + (jnp.dot(x_hi, w_lo, preferred_element_type=F32) + jnp.dot(x_lo, w_hi, preferred_element_type=F32))
              + b_ref[...])
    lane = lax.broadcasted_iota(jnp.int32, logits.shape, 1)
    big = jnp.int32(LANES)
    g_logits = jnp.where(lane < N_GROUPS, logits, NEG_INF)
    g_max = jnp.max(g_logits, axis=-1, keepdims=True)
    g_prob = 1.0 / jnp.sum(jnp.exp(g_logits - g_max), axis=-1, keepdims=True)
    g_idx = jnp.min(jnp.where(g_logits == g_max, lane, big), axis=-1, keepdims=True)
    first = N_GROUPS + g_idx * EXPERTS_PER_GROUP
    e_logits = jnp.where((lane >= first) & (lane < first + EXPERTS_PER_GROUP), logits, NEG_INF)
    v1 = jnp.max(e_logits, axis=-1, keepdims=True)
    i1 = jnp.min(jnp.where(e_logits == v1, lane, big), axis=-1, keepdims=True)
    rest = jnp.where(lane == i1, NEG_INF, e_logits)
    v2 = jnp.max(rest, axis=-1, keepdims=True)
    i2 = jnp.min(jnp.where(rest == v2, lane, big), axis=-1, keepdims=True)
    e2 = jnp.exp(v2 - v1)
    w1 = (1.0 / (1.0 + e2)) * g_prob
    w2 = (e2 / (1.0 + e2)) * g_prob
    @pl.when(pl.program_id(0) == 0)
    def _():
        count_ref[...] = jnp.zeros_like(count_ref)

    hot1 = jnp.where(lane == i1, 1.0, 0.0)
    hot2 = jnp.where(lane == i2, 1.0, 0.0)
    both = hot1 + hot2
    before = _cumsum_rows(both) - both + count_ref[...]
    rank1 = jnp.sum(before * hot1, axis=-1, keepdims=True)
    rank2 = jnp.sum(before * hot2, axis=-1, keepdims=True)
    count_ref[...] = count_ref[...] + jnp.sum(both, axis=0, keepdims=True)

    out = jnp.where(lane == 0, (i1 - N_GROUPS).astype(F32), 0.0)
    out = jnp.where(lane == 1, (i2 - N_GROUPS).astype(F32), out)
    out = jnp.where(lane == 2, w1, out)
    out = jnp.where(lane == 3, w2, out)
    out = jnp.where(lane == 4, rank1, out)
    out = jnp.where(lane == 5, rank2, out)
    o_ref[...] = out


def _router(x, w_cat, b_cat, tm=512):
    t, d = x.shape
    return pl.pallas_call(
        _router_kernel,
        grid=(t // tm,),
        in_specs=[pl.BlockSpec((tm, d), lambda i: (i, 0)),
                  pl.BlockSpec((d, LANES), lambda i: (0, 0)),
                  pl.BlockSpec((1, LANES), lambda i: (0, 0))],
        out_specs=[pl.BlockSpec((tm, LANES), lambda i: (i, 0)), pl.BlockSpec((1, LANES), lambda i: (0, 0))],
        out_shape=[jax.ShapeDtypeStruct((t, LANES), F32), jax.ShapeDtypeStruct((1, LANES), F32)],
        compiler_params=_params("arbitrary"),
        name="moe_router",
    )(x, w_cat, b_cat)


def _plan_kernel(pos_ref, src_ref, dst_ref, *, n_tokens):
    def body(a, c):
        p = pos_ref[a]
        src_ref[p] = jnp.where(a >= n_tokens, a - n_tokens, a)
        dst_ref[p] = a
        return c

    lax.fori_loop(0, 2 * n_tokens, body, 0, unroll=8)


def _plan(pos, n_tokens):
    smem = pl.BlockSpec(memory_space=pltpu.SMEM)
    return pl.pallas_call(
        functools.partial(_plan_kernel, n_tokens=n_tokens),
        in_specs=[smem],
        out_specs=[smem, smem],
        out_shape=[jax.ShapeDtypeStruct((2 * n_tokens,), jnp.int32)] * 2,
        name="moe_plan",
    )(pos)


def _for_rows(n, tm, fn):
    def body(i, c):
        fn(i)
        return c

    @pl.when(n == tm)
    def _():
        for i in range(tm):
            fn(i)

    @pl.when(n < tm)
    def _():
        lax.fori_loop(0, n, body, 0)


def _experts_kernel(tile_e_ref, tile_start_ref, tile_n_ref, src_ref, dst_ref,
                    x_hbm, wg_ref, wu_ref, wd_ref, y_hbm, xbuf, ybuf, gsem, ssem, *, tm, n_tiles):
    del tile_e_ref
    t = pl.program_id(0)
    slot = t % 2

    def gather_row(tile, s, i):
        row = src_ref[tile_start_ref[tile] + i]
        return pltpu.make_async_copy(x_hbm.at[pl.ds(row, 1)], xbuf.at[s, pl.ds(i, 1)], gsem.at[s])

    def scatter_row(tile, s, i):
        row = dst_ref[tile_start_ref[tile] + i]
        return pltpu.make_async_copy(ybuf.at[s, pl.ds(i, 1)], y_hbm.at[pl.ds(row, 1)], ssem.at[s])

    def start_gather(tile, s):
        _for_rows(tile_n_ref[tile], tm, lambda i: gather_row(tile, s, i).start())

    def wait_rows(tile, bulk_copy, row_copy):
        n = tile_n_ref[tile]
        n8 = pl.multiple_of(n // SUBLANES * SUBLANES, SUBLANES)

        @pl.when(n8 > 0)
        def _():
            bulk_copy(n8).wait()

        def tail(i, c):
            row_copy(i).wait()
            return c
        lax.fori_loop(n8, n, tail, 0)

    def wait_gather(tile, s):
        wait_rows(tile,
                  lambda n8: pltpu.make_async_copy(x_hbm.at[pl.ds(0, n8)], xbuf.at[s, pl.ds(0, n8)], gsem.at[s]),
                  lambda i: gather_row(tile, s, i))

    def wait_scatter(tile, s):
        wait_rows(tile,
                  lambda n8: pltpu.make_async_copy(ybuf.at[s, pl.ds(0, n8)], y_hbm.at[pl.ds(0, n8)], ssem.at[s]),
                  lambda i: scatter_row(tile, s, i))

    @pl.when(t == 0)
    def _():
        xbuf[...] = jnp.zeros_like(xbuf)
        start_gather(0, 0)

    @pl.when(t + 1 < n_tiles)
    def _():
        start_gather(t + 1, 1 - slot)

    @pl.when(t >= 2)
    def _():
        wait_scatter(t - 2, slot)

    n_rows = tile_n_ref[t]

    @pl.when(n_rows > 0)
    def _():
        wait_gather(t, slot)
        xb = xbuf[slot].astype(BF16)
        gate = jnp.dot(xb, wg_ref[0, 0, 0].astype(BF16), preferred_element_type=F32)
        up = jnp.dot(xb, wu_ref[0, 0, 0].astype(BF16), preferred_element_type=F32)
        hidden = (_silu(gate) * up).astype(BF16)
        ybuf[slot] = jnp.dot(hidden, wd_ref[0, 0, 0].astype(BF16), preferred_element_type=F32)
        _for_rows(n_rows, tm, lambda i: scatter_row(t, slot, i).start())

    @pl.when(t == n_tiles - 1)
    def _():
        if n_tiles >= 2:
            wait_scatter(t - 1, 1 - slot)
        wait_scatter(t, slot)


def _experts(x, w_gate, w_up, w_down, layer, tile_e, tile_start, tile_n, src_rows, dst_rows, tm, n_tiles):
    t_tokens, d = x.shape
    ff = w_gate.shape[-1]
    w_map = lambda t, te, ts, tn, sr, ds_: (layer, te[t] // EXPERTS_PER_GROUP, te[t] % EXPERTS_PER_GROUP, 0, 0)
    grid_spec = pltpu.PrefetchScalarGridSpec(
        num_scalar_prefetch=5,
        grid=(n_tiles,),
        in_specs=[
            pl.BlockSpec(memory_space=pl.ANY),
            pl.BlockSpec((1, 1, 1, d, ff), w_map),
            pl.BlockSpec((1, 1, 1, d, ff), w_map),
            pl.BlockSpec((1, 1, 1, ff, d), w_map),
        ],
        out_specs=pl.BlockSpec(memory_space=pl.ANY),
        scratch_shapes=[
            pltpu.VMEM((2, tm, d), F32),
            pltpu.VMEM((2, tm, d), F32),
            pltpu.SemaphoreType.DMA((2,)),
            pltpu.SemaphoreType.DMA((2,)),
        ],
    )
    return pl.pallas_call(
        functools.partial(_experts_kernel, tm=tm, n_tiles=n_tiles),
        grid_spec=grid_spec,
        out_shape=jax.ShapeDtypeStruct((2 * t_tokens, d), F32),
        compiler_params=_params("arbitrary"),
        name="moe_experts",
    )(tile_e, tile_start, tile_n, src_rows, dst_rows, x, w_gate, w_up, w_down)


def _combine_ln_kernel(x_ref, y0_ref, y1_ref, r_ref, g_ref, b_ref, o_ref, o_bf_ref):
    w0 = r_ref[:, 2:3]
    w1 = r_ref[:, 3:4]
    y = DN_ALPHA * x_ref[...] + (w0 * y0_ref[...] + w1 * y1_ref[...])
    out = _layer_norm(y, g_ref[...], b_ref[...])
    o_ref[...] = out
    o_bf_ref[...] = out.astype(BF16)


def _combine_ln(x, y2, route, g, b, tm=512):
    t, d = x.shape
    n_t = t // tm
    return pl.pallas_call(
        _combine_ln_kernel,
        grid=(n_t,),
        in_specs=[
            pl.BlockSpec((tm, d), lambda i: (i, 0)),
            pl.BlockSpec((tm, d), lambda i: (i, 0)),
            pl.BlockSpec((tm, d), lambda i: (i + n_t, 0)),
            pl.BlockSpec((tm, LANES), lambda i: (i, 0)),
            pl.BlockSpec((1, d), lambda i: (0, 0)),
            pl.BlockSpec((1, d), lambda i: (0, 0)),
        ],
        out_specs=[pl.BlockSpec((tm, d), lambda i: (i, 0)), pl.BlockSpec((tm, d), lambda i: (i, 0))],
        out_shape=[jax.ShapeDtypeStruct((t, d), F32), jax.ShapeDtypeStruct((t, d), BF16)],
        compiler_params=_params("parallel"),
        name="moe_combine_ln",
    )(x, y2, y2, route, g.reshape(1, d), b.reshape(1, d))


def _moe_layer(x, wg, bg, we, be, w_gate, w_up, w_down, layer, ln_g, ln_b, tm=256):
    t, d = x.shape
    w_cat = jnp.concatenate([wg, jnp.transpose(we, (1, 0, 2)).reshape(d, N_EXPERTS)], axis=1)
    w_cat = jnp.pad(w_cat, ((0, 0), (0, LANES - w_cat.shape[1])))
    b_cat = jnp.pad(jnp.concatenate([bg, be.reshape(-1)]), (0, LANES - N_GROUPS - N_EXPERTS)).reshape(1, LANES)
    route, lane_counts = _router(x, w_cat, b_cat)

    i32 = jnp.int32
    expert = jnp.concatenate([route[:, 0], route[:, 1]]).astype(i32)
    rank = jnp.concatenate([route[:, 4], route[:, 5]]).astype(i32)
    e_ids = jnp.arange(N_EXPERTS, dtype=i32)
    onehot = (expert[:, None] == e_ids[None, :]).astype(i32)
    counts = lane_counts[0, N_GROUPS:N_GROUPS + N_EXPERTS].astype(i32)
    seg_start = jnp.cumsum(counts) - counts
    pos = jnp.sum(onehot * seg_start[None, :], axis=1) + rank
    tiles_per = (counts + tm - 1) // tm
    tile_end = jnp.cumsum(tiles_per)
    tile_begin = tile_end - tiles_per
    n_tiles = (2 * t) // tm + N_EXPERTS
    tile_ids = jnp.arange(n_tiles, dtype=i32)
    n_used = tile_end[-1]
    tile_e = jnp.sum((tile_end[None, :] <= jnp.minimum(tile_ids, n_used - 1)[:, None]).astype(i32), axis=1)
    tile_e = jnp.clip(tile_e, 0, N_EXPERTS - 1)
    te_onehot = (tile_e[:, None] == e_ids[None, :]).astype(i32)
    within = (tile_ids - jnp.sum(te_onehot * tile_begin[None, :], axis=1)) * tm
    tile_n = jnp.clip(jnp.sum(te_onehot * counts[None, :], axis=1) - within, 0, tm)
    tile_n = jnp.where(tile_ids < n_used, tile_n, 0).astype(i32)
    tile_start = jnp.where(tile_n > 0, jnp.sum(te_onehot * seg_start[None, :], axis=1) + within, 0).astype(i32)
    src_rows, dst_rows = _plan(pos.astype(i32), t)
    y2 = _experts(x, w_gate, w_up, w_down, layer, tile_e.astype(i32), tile_start, tile_n, src_rows, dst_rows,
                  tm, n_tiles)
    return _combine_ln(x, y2, route, ln_g, ln_b)


def _even_mixer_layer(x, xb, batch, seq_len, w_in, gate_w2, gate_b, norm_g, fox_gate_b, w_out, ln_g, ln_b):
    gla_w = 2 * GLA_HEADS * GLA_DK + 2 * GLA_HEADS * GLA_DV
    fox_w = 3 * FOX_HEADS * FOX_DH
    glr0 = gla_w
    fox0 = glr0 + GLA_GATE_RANK
    ff0 = fox0 + fox_w
    main = _project(xb, w_in, 0, gla_w, F32, seq_len)
    fqkv = _project(xb, w_in[:, fox0:ff0], 0, fox_w, BF16, seq_len)
    w_small = jnp.concatenate([w_in[:, ff0:ff0 + FOX_HEADS], w_in[:, glr0:fox0]], axis=1)
    w_small = jnp.pad(w_small, ((0, 0), (0, LANES - w_small.shape[1])))
    small = _project(xb, w_small, 0, LANES, F32, seq_len)
    o_gla = _gla(main, small, gate_w2, gate_b, norm_g, batch, seq_len)
    gate_b_row = jnp.pad(fox_gate_b, (0, LANES - FOX_HEADS)).reshape(1, LANES)
    c, c_t = _fox_gate(small, gate_b_row, batch, seq_len)
    fox_heads_per_step = 4
    cq = c.reshape(batch, seq_len, LANES)[:, :, :FOX_HEADS]
    cq = jnp.transpose(cq.reshape(batch, seq_len, FOX_HEADS // fox_heads_per_step, fox_heads_per_step), (0, 2, 1, 3))
    ck = c_t[:, :FOX_HEADS, :].reshape(batch * FOX_HEADS, 1, seq_len)
    o_fox = _fox(fqkv, cq, ck, batch, seq_len, heads=fox_heads_per_step)
    return _out_proj_ln(o_gla, o_fox, 0, _to_bf16(w_out), x, ln_g, ln_b)


def _odd_mixer_layer(x, xb, batch, seq_len, w_in, w_out, ln_g, ln_b):
    qw = DSA_Q_HEADS * DSA_DH
    kw = DSA_KV_HEADS * DSA_DH
    iqw = IDX_HEADS * IDX_DH
    rope_attn = (DSA_DH,) + _rope_tables(seq_len, DSA_DH)
    rope_idx = (IDX_DH,) + _rope_tables(seq_len, IDX_DH)
    qk = _project(xb, w_in, 0, qw + kw, BF16, seq_len, rope=rope_attn)
    v = _project(xb, w_in, qw + kw, kw, BF16, seq_len)
    iq = _project(xb, w_in, qw + 2 * kw, iqw, BF16, seq_len, rope=rope_idx)
    small0 = qw + 2 * kw + iqw
    w_small = jnp.pad(w_in[:, small0:], ((0, 0), (0, LANES - (w_in.shape[1] - small0))))
    small = _project(xb, w_small, 0, LANES, F32, seq_len, rope=rope_idx, rope_lanes=IDX_DH)
    o = _dsa(qk, v, iq, small, batch, seq_len)
    return _out_proj_ln(o, o, 1, _to_bf16(w_out), x, ln_g, ln_b)


def kernel(x, a_w_in, a_gla_gate_w2, a_gla_gate_b, a_gla_norm_g, a_fox_gate_b, a_w_out, c_w_in, c_w_out,
           ln_mix_g, ln_mix_b, ln_ffn_g, ln_ffn_b, moe_group_w, moe_group_b, moe_expert_w, moe_expert_b,
           moe_w_gate, moe_w_up, moe_w_down):
    batch, seq_len, d = x.shape
    h = x.reshape(batch * seq_len, d)
    hb = _to_bf16(h)
    for layer in range(DEPTH):
        j = layer // 2
        if layer % 2 == 0:
            h, hb = _even_mixer_layer(h, hb, batch, seq_len, a_w_in[j], a_gla_gate_w2[j], a_gla_gate_b[j],
                                      a_gla_norm_g[j], a_fox_gate_b[j], a_w_out[j],
                                      ln_mix_g[layer], ln_mix_b[layer])
        else:
            h, hb = _odd_mixer_layer(h, hb, batch, seq_len, c_w_in[j], c_w_out[j],
                                     ln_mix_g[layer], ln_mix_b[layer])
        h, hb = _moe_layer(h, moe_group_w[layer], moe_group_b[layer], moe_expert_w[layer], moe_expert_b[layer],
                           moe_w_gate, moe_w_up, moe_w_down, layer, ln_ffn_g[layer], ln_ffn_b[layer])
    return h.reshape(batch, seq_len, d)
```

```python
import functools

import jax
import jax.numpy as jnp
import numpy as np
from jax import lax
from jax.experimental import pallas as pl
from jax.experimental.pallas import tpu as pltpu

F32 = jnp.float32
BF16 = jnp.bfloat16

DEPTH = 2
GLA_HEADS = 4
GLA_DK = 128
GLA_DV = 256
GLA_GATE_RANK = 16
GLA_GATE_TAU = 16.0
GLA_CHUNK = 64
FOX_HEADS = 8
FOX_DH = 128
DSA_Q_HEADS = 16
DSA_KV_HEADS = 4
DSA_DH = 128
IDX_HEADS = 16
IDX_DH = 64
DSA_TOPK_MAX = 256
N_GROUPS = 4
EXPERTS_PER_GROUP = 8
N_EXPERTS = N_GROUPS * EXPERTS_PER_GROUP
ROPE_THETA = 10000.0
LN_EPS = 1e-5
DN_ALPHA = (2 * DEPTH) ** 0.25

LANES = 128
SUBLANES = 8
VMEM_LIMIT_BYTES = 56 * 1024 * 1024

NEG_INF = float("-inf")
LOG2_E = 1.4426950408889634
MASK_BIAS = -1e30


def _params(*semantics):
    return pltpu.CompilerParams(dimension_semantics=semantics, vmem_limit_bytes=VMEM_LIMIT_BYTES)


def _nt_dot(a, b):
    return lax.dot_general(a, b, (((1,), (1,)), ((), ())), preferred_element_type=F32)


def _tn_dot(a, b):
    return lax.dot_general(a, b, (((0,), (0,)), ((), ())), preferred_element_type=F32)


def _log_sigmoid(z):
    return jnp.minimum(z, 0.0) - jnp.log1p(jnp.exp(-jnp.abs(z)))


def _silu(z):
    return z * (1.0 / (1.0 + jnp.exp(-z)))


def _cumsum_rows(x):
    n = x.shape[0]
    row = lax.broadcasted_iota(jnp.int32, x.shape, 0)
    shift = 1
    while shift < n:
        x = x + jnp.where(row >= shift, pltpu.roll(x, shift, 0), 0.0)
        shift *= 2
    return x


def _rot_half(x, head_dim):
    half = head_dim // 2
    n = x.shape[-1]
    lane = lax.broadcasted_iota(jnp.int32, x.shape, 1)
    fwd = pltpu.roll(x, n - half, 1)
    bwd = pltpu.roll(x, half, 1)
    return jnp.where((lane % head_dim) < half, fwd, bwd)


def _layer_norm(y, g, b):
    mu = jnp.mean(y, axis=-1, keepdims=True)
    d = y - mu
    var = jnp.mean(d * d, axis=-1, keepdims=True)
    return d * lax.rsqrt(var + LN_EPS) * g + b


def _cast_kernel(x_ref, o_ref):
    o_ref[...] = x_ref[...].astype(o_ref.dtype)


def _to_bf16(a, rows=512):
    r, c = a.shape
    rows = min(rows, r)
    return pl.pallas_call(
        _cast_kernel,
        grid=(r // rows,),
        in_specs=[pl.BlockSpec((rows, c), lambda i: (i, 0))],
        out_specs=pl.BlockSpec((rows, c), lambda i: (i, 0)),
        out_shape=jax.ShapeDtypeStruct((r, c), BF16),
        compiler_params=_params("parallel"),
        name="to_bf16",
    )(a)


def _proj_kernel(*refs, rope_dim, rope_lanes):
    if rope_dim:
        x_ref, w_ref, cos_ref, sin_ref, o_ref = refs
    else:
        x_ref, w_ref, o_ref = refs
    acc = _nt_dot(x_ref[...], w_ref[...].astype(BF16))
    if rope_dim:
        reps = acc.shape[-1] // LANES
        cos = jnp.tile(cos_ref[...], (1, reps))
        sin = jnp.tile(sin_ref[...], (1, reps))
        roped = acc * cos + _rot_half(acc, rope_dim) * sin
        if rope_lanes < acc.shape[-1]:
            lane = lax.broadcasted_iota(jnp.int32, acc.shape, 1)
            roped = jnp.where(lane < rope_lanes, roped, acc)
        acc = roped
    o_ref[...] = acc.astype(o_ref.dtype)


def _project(x, wt, col_start, n_cols, out_dtype, seq_len, rope=None, rope_lanes=None, tm=1024, tn=512):
    t, k = x.shape
    tm = min(tm, seq_len)
    tn = min(tn, n_cols)
    assert t % tm == 0 and n_cols % tn == 0 and col_start % tn == 0 and seq_len % tm == 0
    col_off = col_start // tn
    in_specs = [
        pl.BlockSpec((tm, k), lambda i, j: (i, 0)),
        pl.BlockSpec((tn, k), lambda i, j: (j + col_off, 0)),
    ]
    args = [x, wt]
    rope_dim = 0
    if rope is not None:
        rope_dim, cos, sin = rope
        seq_tiles = seq_len // tm
        in_specs += [pl.BlockSpec((tm, LANES), lambda i, j: (i % seq_tiles, 0))] * 2
        args += [cos, sin]
    return pl.pallas_call(
        functools.partial(_proj_kernel, rope_dim=rope_dim, rope_lanes=rope_lanes or n_cols),
        grid=(t // tm, n_cols // tn),
        in_specs=in_specs,
        out_specs=pl.BlockSpec((tm, tn), lambda i, j: (i, j)),
        out_shape=jax.ShapeDtypeStruct((t, n_cols), out_dtype),
        compiler_params=_params("parallel", "parallel"),
        name="in_proj",
    )(*args)


def _rope_tables(seq_len, head_dim):
    half = head_dim // 2
    inv = ROPE_THETA ** (-jnp.arange(half, dtype=F32) / half)
    ang = jnp.arange(seq_len, dtype=F32)[:, None] * inv[None, :]
    cos, sin = jnp.cos(ang), jnp.sin(ang)
    reps = LANES // head_dim
    cos_full = jnp.tile(jnp.concatenate([cos, cos], axis=-1), (1, reps))
    sin_signed = jnp.tile(jnp.concatenate([-sin, sin], axis=-1), (1, reps))
    return cos_full, sin_signed


def _out_proj_ln_kernel(oa_ref, ob_ref, w_ref, x_ref, g_ref, b_ref, out_ref, out_bf_ref):
    half = oa_ref.shape[1]
    h = jnp.dot(oa_ref[...], w_ref[0:half, :], preferred_element_type=F32)
    h = h + jnp.dot(ob_ref[...], w_ref[half:2 * half, :], preferred_element_type=F32)
    out = _layer_norm(DN_ALPHA * x_ref[...] + h, g_ref[...], b_ref[...])
    out_ref[...] = out
    out_bf_ref[...] = out.astype(BF16)


def _out_proj_ln(o_a, o_b, b_col_block, w_bf16, x, g, b, tm=512):
    t = o_a.shape[0]
    kdim, d = w_bf16.shape
    half = kdim // 2
    tm = min(tm, t)
    return pl.pallas_call(
        _out_proj_ln_kernel,
        grid=(t // tm,),
        in_specs=[
            pl.BlockSpec((tm, half), lambda i: (i, 0)),
            pl.BlockSpec((tm, half), lambda i: (i, b_col_block)),
            pl.BlockSpec((kdim, d), lambda i: (0, 0)),
            pl.BlockSpec((tm, d), lambda i: (i, 0)),
            pl.BlockSpec((1, d), lambda i: (0, 0)),
            pl.BlockSpec((1, d), lambda i: (0, 0)),
        ],
        out_specs=[pl.BlockSpec((tm, d), lambda i: (i, 0)), pl.BlockSpec((tm, d), lambda i: (i, 0))],
        out_shape=[jax.ShapeDtypeStruct((t, d), F32), jax.ShapeDtypeStruct((t, d), BF16)],
        compiler_params=_params("parallel"),
        name="out_proj_ln",
    )(o_a, o_b, w_bf16, x, g.reshape(1, d), b.reshape(1, d))


def _gla_kernel(q_ref, k_ref, v_ref, r_ref, sm_ref, w2_ref, gb_ref, ng_ref, o_ref, state_ref, *, n_chunks):
    c_len, dk, dv = GLA_CHUNK, GLA_DK, GLA_DV

    @pl.when(pl.program_id(1) == 0)
    def _():
        state_ref[...] = jnp.zeros_like(state_ref)

    row = lax.broadcasted_iota(jnp.int32, (c_len, c_len), 0)
    col = lax.broadcasted_iota(jnp.int32, (c_len, c_len), 1)
    causal = row >= col
    norm_g = ng_ref[...]
    for c in range(n_chunks):
        rows = slice(c * c_len, (c + 1) * c_len)
        glr = sm_ref[rows, FOX_HEADS:FOX_HEADS + GLA_GATE_RANK].astype(BF16)
        for h in range(GLA_HEADS):
            kcols = slice(h * dk, (h + 1) * dk)
            vcols = slice(h * dv, (h + 1) * dv)
            z = jnp.dot(glr, w2_ref[:, kcols].astype(BF16), preferred_element_type=F32) + gb_ref[:, kcols]
            log_a = _log_sigmoid(z) / GLA_GATE_TAU
            b = _cumsum_rows(log_a)
            b_last = b[c_len - 1:c_len, :]
            q = q_ref[rows, kcols] * (dk ** -0.5)
            k = k_ref[rows, kcols]
            v = v_ref[rows, vcols].astype(BF16)
            q_dec = (q * jnp.exp(b)).astype(BF16)
            k_inv = (k * jnp.exp(-b)).astype(BF16)
            k_end = (k * jnp.exp(b_last - b)).astype(BF16)
            scores = jnp.where(causal, _nt_dot(q_dec, k_inv), 0.0)
            state_t = state_ref[h]
            o = jnp.dot(scores.astype(BF16), v, preferred_element_type=F32)
            o = o + _nt_dot(q_dec, state_t.astype(BF16))
            state_ref[h] = jnp.exp(b_last) * state_t + _tn_dot(v, k_end)
            rms = lax.rsqrt(jnp.mean(o * o, axis=-1, keepdims=True) + LN_EPS)
            o_ref[rows, vcols] = (o * rms * norm_g * _silu(r_ref[rows, vcols])).astype(o_ref.dtype)


def _gla(main, small, gate_w2, gate_b, norm_g, batch, seq_len, rows_per_step=256):
    t = main.shape[0]
    n_steps = seq_len // rows_per_step
    hk, hv = GLA_HEADS * GLA_DK, GLA_HEADS * GLA_DV
    row_map = lambda b, n: b * n_steps + n
    return pl.pallas_call(
        functools.partial(_gla_kernel, n_chunks=rows_per_step // GLA_CHUNK),
        grid=(batch, n_steps),
        in_specs=[
            pl.BlockSpec((rows_per_step, hk), lambda b, n: (row_map(b, n), 0)),
            pl.BlockSpec((rows_per_step, hk), lambda b, n: (row_map(b, n), 1)),
            pl.BlockSpec((rows_per_step, hv), lambda b, n: (row_map(b, n), 1)),
            pl.BlockSpec((rows_per_step, hv), lambda b, n: (row_map(b, n), 2)),
            pl.BlockSpec((rows_per_step, LANES), lambda b, n: (row_map(b, n), 0)),
            pl.BlockSpec((GLA_GATE_RANK, hk), lambda b, n: (0, 0)),
            pl.BlockSpec((1, hk), lambda b, n: (0, 0)),
            pl.BlockSpec((1, GLA_DV), lambda b, n: (0, 0)),
        ],
        out_specs=pl.BlockSpec((rows_per_step, hv), lambda b, n: (row_map(b, n), 0)),
        out_shape=jax.ShapeDtypeStruct((t, hv), BF16),
        scratch_shapes=[pltpu.VMEM((GLA_HEADS, GLA_DV, GLA_DK), F32)],
        compiler_params=_params("parallel", "arbitrary"),
        name="gla",
    )(main, main, main, main, small, gate_w2, gate_b.reshape(1, hk), norm_g.reshape(1, GLA_DV))


def _fox_gate_kernel(sm_ref, b_ref, c_ref, ct_ref):
    c = _cumsum_rows(_log_sigmoid(sm_ref[...] + b_ref[...]))
    c_ref[...] = c
    ct_ref[0] = c.T


def _fox_gate(small, gate_b_row, batch, seq_len):
    t = small.shape[0]
    return pl.pallas_call(
        _fox_gate_kernel,
        grid=(batch,),
        in_specs=[pl.BlockSpec((seq_len, LANES), lambda b: (b, 0)), pl.BlockSpec((1, LANES), lambda b: (0, 0))],
        out_specs=[pl.BlockSpec((seq_len, LANES), lambda b: (b, 0)),
                   pl.BlockSpec((1, LANES, seq_len), lambda b: (b, 0, 0))],
        out_shape=[jax.ShapeDtypeStruct((t, LANES), F32), jax.ShapeDtypeStruct((batch, LANES, seq_len), F32)],
        compiler_params=_params("parallel"),
        name="fox_gate",
    )(small, gate_b_row)


def _fox_kernel(q_ref, k_ref, v_ref, cq_ref, ck_ref, o_ref, m_ref, l_ref, acc_ref, *, tq, heads):
    qi = pl.program_id(2)
    dh = FOX_DH
    scale = dh ** -0.5
    inv_scale = dh ** 0.5
    c_exp = scale * LOG2_E
    reps = tq // LANES

    m_ref[...] = jnp.full_like(m_ref, MASK_BIAS)
    l_ref[...] = jnp.zeros_like(l_ref)
    acc_ref[...] = jnp.zeros_like(acc_ref)

    def step(j, masked):
        ks = pl.multiple_of(j * tq, tq)
        for h in range(heads):
            cols = slice(h * dh, (h + 1) * dh)
            q = q_ref[:, cols]
            k = k_ref[pl.ds(ks, tq), cols]
            v = v_ref[pl.ds(ks, tq), cols]
            cq = cq_ref[0, 0, :, h:h + 1] * inv_scale
            s = _nt_dot(q, k) - ck_ref[h, :, pl.ds(ks, tq)] * inv_scale
            if masked:
                row = lax.broadcasted_iota(jnp.int32, s.shape, 0)
                col = lax.broadcasted_iota(jnp.int32, s.shape, 1)
                s = jnp.where(col <= row, s, MASK_BIAS)
            m_prev = m_ref[h]
            m_new = jnp.maximum(m_prev, jnp.max(s, axis=-1, keepdims=True) + cq)
            shift = m_new - cq
            p = jnp.exp2((s - jnp.tile(shift, (1, reps))) * c_exp)
            alpha = jnp.exp2((m_prev - m_new) * c_exp)
            p_sum = p[:, 0:LANES]
            for r in range(1, reps):
                p_sum = p_sum + p[:, r * LANES:(r + 1) * LANES]
            l_ref[h] = alpha * l_ref[h] + p_sum
            acc_ref[h] = alpha * acc_ref[h] + jnp.dot(p.astype(BF16), v, preferred_element_type=F32)
            m_ref[h] = m_new

    def body(j, carry):
        step(j, False)
        return carry

    lax.fori_loop(0, qi, body, 0)
    step(qi, True)
    for h in range(heads):
        l = jnp.sum(l_ref[h], axis=-1, keepdims=True)
        o_ref[:, h * dh:(h + 1) * dh] = (acc_ref[h] / l).astype(o_ref.dtype)


def _fox(fqkv, cq, ck, batch, seq_len, tq=1024, heads=4):
    t = fqkv.shape[0]
    tq = min(tq, seq_len)
    assert seq_len % tq == 0
    nq = seq_len // tq
    n_hg = FOX_HEADS // heads
    w = heads * FOX_DH
    return pl.pallas_call(
        functools.partial(_fox_kernel, tq=tq, heads=heads),
        grid=(batch, n_hg, nq),
        in_specs=[
            pl.BlockSpec((tq, w), lambda b, g, i: (b * nq + i, g)),
            pl.BlockSpec((seq_len, w), lambda b, g, i: (b, n_hg + g)),
            pl.BlockSpec((seq_len, w), lambda b, g, i: (b, 2 * n_hg + g)),
            pl.BlockSpec((1, 1, tq, heads), lambda b, g, i: (b, g, i, 0)),
            pl.BlockSpec((heads, 1, seq_len), lambda b, g, i: (b * n_hg + g, 0, 0)),
        ],
        out_specs=pl.BlockSpec((tq, w), lambda b, g, i: (b * nq + i, g)),
        out_shape=jax.ShapeDtypeStruct((t, FOX_HEADS * FOX_DH), BF16),
        scratch_shapes=[
            pltpu.VMEM((heads, tq, LANES), F32),
            pltpu.VMEM((heads, tq, LANES), F32),
            pltpu.VMEM((heads, tq, FOX_DH), F32),
        ],
        compiler_params=_params("parallel", "parallel", "arbitrary"),
        name="fox_attn",
    )(fqkv, fqkv, fqkv, cq, ck)


def _dsa_kernel(q_ref, k_ref, v_ref, iq_ref, ik_ref, iw_ref, o_ref, sc_ref, m_ref, l_ref, acc_ref,
                *, tq, tk, top_k, max_iters):
    qi = pl.program_id(1)
    n_kc = (qi * tq + tq + tk - 1) // tk
    group = DSA_Q_HEADS // DSA_KV_HEADS
    qpos = qi * tq + lax.broadcasted_iota(jnp.int32, (tq, 1), 0)

    iq = jnp.concatenate([iq_ref[:, h * IDX_DH:(h + 1) * IDX_DH] for h in range(IDX_HEADS)], axis=0)
    iw = iw_ref[:, IDX_DH:IDX_DH + IDX_HEADS] * (IDX_HEADS ** -0.5 * IDX_DH ** -0.5)

    def score_chunk(c, carry):
        lo, hi = carry
        ks = pl.multiple_of(c * tk, tk)
        ik = ik_ref[pl.ds(ks, tk), 0:IDX_DH].astype(BF16)
        rel = _nt_dot(iq, ik)
        score = jnp.zeros((tq, tk), F32)
        for h in range(IDX_HEADS):
            score = score + iw[:, h:h + 1] * jnp.maximum(rel[h * tq:(h + 1) * tq, :], 0.0)
        kpos = ks + lax.broadcasted_iota(jnp.int32, (tq, tk), 1)
        valid = kpos <= qpos
        sc_ref[:, pl.ds(ks, tk)] = jnp.where(valid, score, NEG_INF)
        lo = jnp.minimum(lo, jnp.min(jnp.where(valid, score, jnp.inf), axis=-1, keepdims=True))
        hi = jnp.maximum(hi, jnp.max(jnp.where(valid, score, NEG_INF), axis=-1, keepdims=True))
        return lo, hi

    lo0, hi0 = lax.fori_loop(0, n_kc, score_chunk,
                             (jnp.full((tq, 1), jnp.inf, F32), jnp.full((tq, 1), NEG_INF, F32)))

    lane_reps = tk // LANES

    def count_ge(thr):
        thr_b = jnp.tile(jnp.broadcast_to(thr, (tq, LANES)), (1, lane_reps))

        def body(c, acc):
            ks = pl.multiple_of(c * tk, tk)
            hit = jnp.where(sc_ref[:, pl.ds(ks, tk)] >= thr_b, 1.0, 0.0)
            for r in range(lane_reps):
                acc = acc + hit[:, r * LANES:(r + 1) * LANES]
            return acc
        partial = lax.fori_loop(0, n_kc, body, jnp.zeros((tq, LANES), F32))
        return jnp.sum(partial, axis=-1, keepdims=True)

    k_f = float(top_k)
    n_keys = (qpos + 1).astype(F32)

    def unsettled(cnt_lo):
        return jnp.max(jnp.where(cnt_lo > k_f, 1.0, 0.0)) > 0.0

    def cond(state):
        it, _, _, cnt_lo = state
        return jnp.logical_and(it < max_iters, unsettled(cnt_lo))

    def refine(state):
        it, lo, hi, cnt_lo = state
        mid = 0.5 * (lo + hi)
        cnt = count_ge(mid)
        keep = cnt >= k_f
        lo = jnp.where(keep, mid, lo)
        cnt_lo = jnp.where(keep, cnt, cnt_lo)
        hi = jnp.where(keep, hi, mid)
        return it + 1, lo, hi, cnt_lo

    _, thr, _, _ = lax.while_loop(cond, refine, (jnp.int32(0), lo0, hi0, n_keys))

    m_ref[...] = jnp.full_like(m_ref, MASK_BIAS)
    l_ref[...] = jnp.zeros_like(l_ref)
    acc_ref[...] = jnp.zeros_like(acc_ref)
    c_exp = DSA_DH ** -0.5 * LOG2_E
    thr_b = jnp.tile(jnp.broadcast_to(thr, (tq, LANES)), (1, lane_reps))

    def attend(c, carry):
        ks = pl.multiple_of(c * tk, tk)
        bias = jnp.where(sc_ref[:, pl.ds(ks, tk)] >= thr_b, 0.0, MASK_BIAS)
        for g in range(DSA_KV_HEADS):
            heads = slice(g * group, (g + 1) * group)
            qg = jnp.concatenate(
                [q_ref[:, (g * group + j) * DSA_DH:(g * group + j + 1) * DSA_DH] for j in range(group)], axis=0)
            kc = k_ref[pl.ds(ks, tk), g * DSA_DH:(g + 1) * DSA_DH]
            vc = v_ref[pl.ds(ks, tk), g * DSA_DH:(g + 1) * DSA_DH]
            s = _nt_dot(qg, kc).reshape(group, tq, tk) + bias[None]
            m_prev = m_ref[heads]
            m_new = jnp.maximum(m_prev, jnp.max(s, axis=-1, keepdims=True))
            p = jnp.exp2((s - jnp.tile(m_new, (1, 1, lane_reps))) * c_exp)
            alpha = jnp.exp2((m_prev - m_new) * c_exp)
            p_sum = p[:, :, 0:LANES]
            for r in range(1, lane_reps):
                p_sum = p_sum + p[:, :, r * LANES:(r + 1) * LANES]
            l_ref[heads] = alpha * l_ref[heads] + p_sum
            pv = jnp.dot(p.reshape(group * tq, tk).astype(BF16), vc, preferred_element_type=F32)
            acc_ref[heads] = alpha * acc_ref[heads] + pv.reshape(group, tq, DSA_DH)
            m_ref[heads] = m_new
        return carry

    lax.fori_loop(0, n_kc, attend, 0)
    for h in range(DSA_Q_HEADS):
        l = jnp.sum(l_ref[h], axis=-1, keepdims=True)
        o_ref[:, h * DSA_DH:(h + 1) * DSA_DH] = (acc_ref[h] / l).astype(o_ref.dtype)


def _dsa(qk, v, iq, small, batch, seq_len, tq=256, tk=512):
    t = qk.shape[0]
    tk = min(tk, seq_len)
    nq = seq_len // tq
    qw = DSA_Q_HEADS * DSA_DH
    kw = DSA_KV_HEADS * DSA_DH
    top_k = min(DSA_TOPK_MAX, seq_len // 4)
    return pl.pallas_call(
        functools.partial(_dsa_kernel, tq=tq, tk=tk, top_k=top_k, max_iters=64),
        grid=(batch, nq),
        in_specs=[
            pl.BlockSpec((tq, qw), lambda b, i: (b * nq + i, 0)),
            pl.BlockSpec((seq_len, kw), lambda b, i: (b, qw // kw)),
            pl.BlockSpec((seq_len, kw), lambda b, i: (b, 0)),
            pl.BlockSpec((tq, IDX_HEADS * IDX_DH), lambda b, i: (b * nq + i, 0)),
            pl.BlockSpec((seq_len, LANES), lambda b, i: (b, 0)),
            pl.BlockSpec((tq, LANES), lambda b, i: (b * nq + i, 0)),
        ],
        out_specs=pl.BlockSpec((tq, qw), lambda b, i: (b * nq + i, 0)),
        out_shape=jax.ShapeDtypeStruct((t, qw), BF16),
        scratch_shapes=[
            pltpu.VMEM((tq, seq_len), F32),
            pltpu.VMEM((DSA_Q_HEADS, tq, LANES), F32),
            pltpu.VMEM((DSA_Q_HEADS, tq, LANES), F32),
            pltpu.VMEM((DSA_Q_HEADS, tq, DSA_DH), F32),
        ],
        compiler_params=_params("parallel", "arbitrary"),
        name="dsa",
    )(qk, qk, v, iq, small, small)


def _router_kernel(x_ref, w_ref, b_ref, o_ref, count_ref):
    x = x_ref[...]
    w = w_ref[...]
    x_hi = x.astype(BF16)
    w_hi = w.astype(BF16)
    x_lo = (x - x_hi.astype(F32)).astype(BF16)
    w_lo = (w - w_hi.astype(F32)).astype(BF16)
    logits = (jnp.dot(x_hi, w_hi, preferred_element_type=F32)
              + (jnp.dot(x_hi, w_lo, preferred_element_type=F32) + jnp.dot(x_lo, w_hi, preferred_element_type=F32))
              + b_ref[...])
    lane = lax.broadcasted_iota(jnp.int32, logits.shape, 1)
    big = jnp.int32(LANES)
    g_logits = jnp.where(lane < N_GROUPS, logits, NEG_INF)
    g_max = jnp.max(g_logits, axis=-1, keepdims=True)
    g_prob = 1.0 / jnp.sum(jnp.exp(g_logits - g_max), axis=-1, keepdims=True)
    g_idx = jnp.min(jnp.where(g_logits == g_max, lane, big), axis=-1, keepdims=True)
    first = N_GROUPS + g_idx * EXPERTS_PER_GROUP
    e_logits = jnp.where((lane >= first) & (lane < first + EXPERTS_PER_GROUP), logits, NEG_INF)
    v1 = jnp.max(e_logits, axis=-1, keepdims=True)
    i1 = jnp.min(jnp.where(e_logits == v1, lane, big), axis=-1, keepdims=True)
    rest = jnp.where(lane == i1, NEG_INF, e_logits)
    v2 = jnp.max(rest, axis=-1, keepdims=True)
    i2 = jnp.min(jnp.where(rest == v2, lane, big), axis=-1, keepdims=True)
    e2 = jnp.exp(v2 - v1)
    w1 = (1.0 / (1.0 + e2)) * g_prob
    w2 = (e2 / (1.0 + e2)) * g_prob
    @pl.when(pl.program_id(0) == 0)
    def _():
        count_ref[...] = jnp.zeros_like(count_ref)

    hot1 = jnp.where(lane == i1, 1.0, 0.0)
    hot2 = jnp.where(lane == i2, 1.0, 0.0)
    both = hot1 + hot2
    before = _cumsum_rows(both) - both + count_ref[...]
    rank1 = jnp.sum(before * hot1, axis=-1, keepdims=True)
    rank2 = jnp.sum(before * hot2, axis=-1, keepdims=True)
    count_ref[...] = count_ref[...] + jnp.sum(both, axis=0, keepdims=True)

    out = jnp.where(lane == 0, (i1 - N_GROUPS).astype(F32), 0.0)
    out = jnp.where(lane == 1, (i2 - N_GROUPS).astype(F32), out)
    out = jnp.where(lane == 2, w1, out)
    out = jnp.where(lane == 3, w2, out)
    out = jnp.where(lane == 4, rank1, out)
    out = jnp.where(lane == 5, rank2, out)
    o_ref[...] = out


def _router(x, w_cat, b_cat, tm=512):
    t, d = x.shape
    return pl.pallas_call(
        _router_kernel,
        grid=(t // tm,),
        in_specs=[pl.BlockSpec((tm, d), lambda i: (i, 0)),
                  pl.BlockSpec((d, LANES), lambda i: (0, 0)),
                  pl.BlockSpec((1, LANES), lambda i: (0, 0))],
        out_specs=[pl.BlockSpec((tm, LANES), lambda i: (i, 0)), pl.BlockSpec((1, LANES), lambda i: (0, 0))],
        out_shape=[jax.ShapeDtypeStruct((t, LANES), F32), jax.ShapeDtypeStruct((1, LANES), F32)],
        compiler_params=_params("arbitrary"),
        name="moe_router",
    )(x, w_cat, b_cat)


def _plan_kernel(pos_ref, src_ref, dst_ref, *, n_tokens):
    def body(a, c):
        p = pos_ref[a]
        src_ref[p] = jnp.where(a >= n_tokens, a - n_tokens, a)
        dst_ref[p] = a
        return c

    lax.fori_loop(0, 2 * n_tokens, body, 0, unroll=8)


def _plan(pos, n_tokens):
    smem = pl.BlockSpec(memory_space=pltpu.SMEM)
    return pl.pallas_call(
        functools.partial(_plan_kernel, n_tokens=n_tokens),
        in_specs=[smem],
        out_specs=[smem, smem],
        out_shape=[jax.ShapeDtypeStruct((2 * n_tokens,), jnp.int32)] * 2,
        name="moe_plan",
    )(pos)


def _for_rows(n, tm, fn):
    def body(i, c):
        fn(i)
        return c

    @pl.when(n == tm)
    def _():
        for i in range(tm):
            fn(i)

    @pl.when(n < tm)
    def _():
        lax.fori_loop(0, n, body, 0)


def _experts_kernel(tile_e_ref, tile_start_ref, tile_n_ref, src_ref, dst_ref,
                    x_hbm, wg_ref, wu_ref, wd_ref, y_hbm, xbuf, ybuf, gsem, ssem, *, tm, n_tiles):
    del tile_e_ref
    t = pl.program_id(0)
    slot = t % 2

    def gather_row(tile, s, i):
        row = src_ref[tile_start_ref[tile] + i]
        return pltpu.make_async_copy(x_hbm.at[pl.ds(row, 1)], xbuf.at[s, pl.ds(i, 1)], gsem.at[s])

    def scatter_row(tile, s, i):
        row = dst_ref[tile_start_ref[tile] + i]
        return pltpu.make_async_copy(ybuf.at[s, pl.ds(i, 1)], y_hbm.at[pl.ds(row, 1)], ssem.at[s])

    def start_gather(tile, s):
        _for_rows(tile_n_ref[tile], tm, lambda i: gather_row(tile, s, i).start())

    def wait_rows(tile, bulk_copy, row_copy):
        n = tile_n_ref[tile]
        n8 = pl.multiple_of(n // SUBLANES * SUBLANES, SUBLANES)

        @pl.when(n8 > 0)
        def _():
            bulk_copy(n8).wait()

        def tail(i, c):
            row_copy(i).wait()
            return c
        lax.fori_loop(n8, n, tail, 0)

    def wait_gather(tile, s):
        wait_rows(tile,
                  lambda n8: pltpu.make_async_copy(x_hbm.at[pl.ds(0, n8)], xbuf.at[s, pl.ds(0, n8)], gsem.at[s]),
                  lambda i: gather_row(tile, s, i))

    def wait_scatter(tile, s):
        wait_rows(tile,
                  lambda n8: pltpu.make_async_copy(ybuf.at[s, pl.ds(0, n8)], y_hbm.at[pl.ds(0, n8)], ssem.at[s]),
                  lambda i: scatter_row(tile, s, i))

    @pl.when(t == 0)
    def _():
        xbuf[...] = jnp.zeros_like(xbuf)
        start_gather(0, 0)

    @pl.when(t + 1 < n_tiles)
    def _():
        start_gather(t + 1, 1 - slot)

    @pl.when(t >= 2)
    def _():
        wait_scatter(t - 2, slot)

    n_rows = tile_n_ref[t]

    @pl.when(n_rows > 0)
    def _():
        wait_gather(t, slot)
        xb = xbuf[slot].astype(BF16)
        gate = jnp.dot(xb, wg_ref[0, 0, 0].astype(BF16), preferred_element_type=F32)
        up = jnp.dot(xb, wu_ref[0, 0, 0].astype(BF16), preferred_element_type=F32)
        hidden = (_silu(gate) * up).astype(BF16)
        ybuf[slot] = jnp.dot(hidden, wd_ref[0, 0, 0].astype(BF16), preferred_element_type=F32)
        _for_rows(n_rows, tm, lambda i: scatter_row(t, slot, i).start())

    @pl.when(t == n_tiles - 1)
    def _():
        if n_tiles >= 2:
            wait_scatter(t - 1, 1 - slot)
        wait_scatter(t, slot)


def _experts(x, w_gate, w_up, w_down, layer, tile_e, tile_start, tile_n, src_rows, dst_rows, tm, n_tiles):
    t_tokens, d = x.shape
    ff = w_gate.shape[-1]
    w_map = lambda t, te, ts, tn, sr, ds_: (layer, te[t] // EXPERTS_PER_GROUP, te[t] % EXPERTS_PER_GROUP, 0, 0)
    grid_spec = pltpu.PrefetchScalarGridSpec(
        num_scalar_prefetch=5,
        grid=(n_tiles,),
        in_specs=[
            pl.BlockSpec(memory_space=pl.ANY),
            pl.BlockSpec((1, 1, 1, d, ff), w_map),
            pl.BlockSpec((1, 1, 1, d, ff), w_map),
            pl.BlockSpec((1, 1, 1, ff, d), w_map),
        ],
        out_specs=pl.BlockSpec(memory_space=pl.ANY),
        scratch_shapes=[
            pltpu.VMEM((2, tm, d), F32),
            pltpu.VMEM((2, tm, d), F32),
            pltpu.SemaphoreType.DMA((2,)),
            pltpu.SemaphoreType.DMA((2,)),
        ],
    )
    return pl.pallas_call(
        functools.partial(_experts_kernel, tm=tm, n_tiles=n_tiles),
        grid_spec=grid_spec,
        out_shape=jax.ShapeDtypeStruct((2 * t_tokens, d), F32),
        compiler_params=_params("arbitrary"),
        name="moe_experts",
    )(tile_e, tile_start, tile_n, src_rows, dst_rows, x, w_gate, w_up, w_down)


def _combine_ln_kernel(x_ref, y0_ref, y1_ref, r_ref, g_ref, b_ref, o_ref, o_bf_ref):
    w0 = r_ref[:, 2:3]
    w1 = r_ref[:, 3:4]
    y = DN_ALPHA * x_ref[...] + (w0 * y0_ref[...] + w1 * y1_ref[...])
    out = _layer_norm(y, g_ref[...], b_ref[...])
    o_ref[...] = out
    o_bf_ref[...] = out.astype(BF16)


def _combine_ln(x, y2, route, g, b, tm=512):
    t, d = x.shape
    n_t = t // tm
    return pl.pallas_call(
        _combine_ln_kernel,
        grid=(n_t,),
        in_specs=[
            pl.BlockSpec((tm, d), lambda i: (i, 0)),
            pl.BlockSpec((tm, d), lambda i: (i, 0)),
            pl.BlockSpec((tm, d), lambda i: (i + n_t, 0)),
            pl.BlockSpec((tm, LANES), lambda i: (i, 0)),
            pl.BlockSpec((1, d), lambda i: (0, 0)),
            pl.BlockSpec((1, d), lambda i: (0, 0)),
        ],
        out_specs=[pl.BlockSpec((tm, d), lambda i: (i, 0)), pl.BlockSpec((tm, d), lambda i: (i, 0))],
        out_shape=[jax.ShapeDtypeStruct((t, d), F32), jax.ShapeDtypeStruct((t, d), BF16)],
        compiler_params=_params("parallel"),
        name="moe_combine_ln",
    )(x, y2, y2, route, g.reshape(1, d), b.reshape(1, d))


def _moe_layer(x, wg, bg, we, be, w_gate, w_up, w_down, layer, ln_g, ln_b, tm=256):
    t, d = x.shape
    w_cat = jnp.concatenate([wg, jnp.transpose(we, (1, 0, 2)).reshape(d, N_EXPERTS)], axis=1)
    w_cat = jnp.pad(w_cat, ((0, 0), (0, LANES - w_cat.shape[1])))
    b_cat = jnp.pad(jnp.concatenate([bg, be.reshape(-1)]), (0, LANES - N_GROUPS - N_EXPERTS)).reshape(1, LANES)
    route, lane_counts = _router(x, w_cat, b_cat)

    i32 = jnp.int32
    expert = jnp.concatenate([route[:, 0], route[:, 1]]).astype(i32)
    rank = jnp.concatenate([route[:, 4], route[:, 5]]).astype(i32)
    e_ids = jnp.arange(N_EXPERTS, dtype=i32)
    onehot = (expert[:, None] == e_ids[None, :]).astype(i32)
    counts = lane_counts[0, N_GROUPS:N_GROUPS + N_EXPERTS].astype(i32)
    seg_start = jnp.cumsum(counts) - counts
    pos = jnp.sum(onehot * seg_start[None, :], axis=1) + rank
    tiles_per = (counts + tm - 1) // tm
    tile_end = jnp.cumsum(tiles_per)
    tile_begin = tile_end - tiles_per
    n_tiles = (2 * t) // tm + N_EXPERTS
    tile_ids = jnp.arange(n_tiles, dtype=i32)
    n_used = tile_end[-1]
    tile_e = jnp.sum((tile_end[None, :] <= jnp.minimum(tile_ids, n_used - 1)[:, None]).astype(i32), axis=1)
    tile_e = jnp.clip(tile_e, 0, N_EXPERTS - 1)
    te_onehot = (tile_e[:, None] == e_ids[None, :]).astype(i32)
    within = (tile_ids - jnp.sum(te_onehot * tile_begin[None, :], axis=1)) * tm
    tile_n = jnp.clip(jnp.sum(te_onehot * counts[None, :], axis=1) - within, 0, tm)
    tile_n = jnp.where(tile_ids < n_used, tile_n, 0).astype(i32)
    tile_start = jnp.where(tile_n > 0, jnp.sum(te_onehot * seg_start[None, :], axis=1) + within, 0).astype(i32)
    src_rows, dst_rows = _plan(pos.astype(i32), t)
    y2 = _experts(x, w_gate, w_up, w_down, layer, tile_e.astype(i32), tile_start, tile_n, src_rows, dst_rows,
                  tm, n_tiles)
    return _combine_ln(x, y2, route, ln_g, ln_b)


def _even_mixer_layer(x, xb, batch, seq_len, w_in, gate_w2, gate_b, norm_g, fox_gate_b, w_out, ln_g, ln_b):
    gla_w = 2 * GLA_HEADS * GLA_DK + 2 * GLA_HEADS * GLA_DV
    fox_w = 3 * FOX_HEADS * FOX_DH
    glr0 = gla_w
    fox0 = glr0 + GLA_GATE_RANK
    ff0 = fox0 + fox_w
    wt = jnp.swapaxes(w_in, 0, 1)
    main = _project(xb, wt, 0, gla_w, F32, seq_len)
    fqkv = _project(xb, wt[fox0:ff0], 0, fox_w, BF16, seq_len)
    wt_small = jnp.concatenate([wt[ff0:ff0 + FOX_HEADS], wt[glr0:fox0]], axis=0)
    wt_small = jnp.pad(wt_small, ((0, LANES - wt_small.shape[0]), (0, 0)))
    small = _project(xb, wt_small, 0, LANES, F32, seq_len)
    o_gla = _gla(main, small, gate_w2, gate_b, norm_g, batch, seq_len)
    gate_b_row = jnp.pad(fox_gate_b, (0, LANES - FOX_HEADS)).reshape(1, LANES)
    c, c_t = _fox_gate(small, gate_b_row, batch, seq_len)
    fox_heads_per_step = 4
    cq = c.reshape(batch, seq_len, LANES)[:, :, :FOX_HEADS]
    cq = jnp.transpose(cq.reshape(batch, seq_len, FOX_HEADS // fox_heads_per_step, fox_heads_per_step), (0, 2, 1, 3))
    ck = c_t[:, :FOX_HEADS, :].reshape(batch * FOX_HEADS, 1, seq_len)
    o_fox = _fox(fqkv, cq, ck, batch, seq_len, heads=fox_heads_per_step)
    return _out_proj_ln(o_gla, o_fox, 0, _to_bf16(w_out), x, ln_g, ln_b)


def _odd_mixer_layer(x, xb, batch, seq_len, w_in, w_out, ln_g, ln_b):
    qw = DSA_Q_HEADS * DSA_DH
    kw = DSA_KV_HEADS * DSA_DH
    iqw = IDX_HEADS * IDX_DH
    rope_attn = (DSA_DH,) + _rope_tables(seq_len, DSA_DH)
    rope_idx = (IDX_DH,) + _rope_tables(seq_len, IDX_DH)
    wt = jnp.swapaxes(w_in, 0, 1)
    qk = _project(xb, wt, 0, qw + kw, BF16, seq_len, rope=rope_attn)
    v = _project(xb, wt, qw + kw, kw, BF16, seq_len)
    iq = _project(xb, wt, qw + 2 * kw, iqw, BF16, seq_len, rope=rope_idx)
    small0 = qw + 2 * kw + iqw
    wt_small = jnp.pad(wt[small0:], ((0, LANES - (wt.shape[0] - small0)), (0, 0)))
    small = _project(xb, wt_small, 0, LANES, F32, seq_len, rope=rope_idx, rope_lanes=IDX_DH)
    o = _dsa(qk, v, iq, small, batch, seq_len)
    return _out_proj_ln(o, o, 1, _to_bf16(w_out), x, ln_g, ln_b)


def kernel(x, a_w_in, a_gla_gate_w2, a_gla_gate_b, a_gla_norm_g, a_fox_gate_b, a_w_out, c_w_in, c_w_out,
           ln_mix_g, ln_mix_b, ln_ffn_g, ln_ffn_b, moe_group_w, moe_group_b, moe_expert_w, moe_expert_b,
           moe_w_gate, moe_w_up, moe_w_down):
    batch, seq_len, d = x.shape
    h = x.reshape(batch * seq_len, d)
    hb = _to_bf16(h)
    for layer in range(DEPTH):
        j = layer // 2
        if layer % 2 == 0:
            h, hb = _even_mixer_layer(h, hb, batch, seq_len, a_w_in[j], a_gla_gate_w2[j], a_gla_gate_b[j],
                                      a_gla_norm_g[j], a_fox_gate_b[j], a_w_out[j],
                                      ln_mix_g[layer], ln_mix_b[layer])
        else:
            h, hb = _odd_mixer_layer(h, hb, batch, seq_len, c_w_in[j], c_w_out[j],
                                     ln_mix_g[layer], ln_mix_b[layer])
        h, hb = _moe_layer(h, moe_group_w[layer], moe_group_b[layer], moe_expert_w[layer], moe_expert_b[layer],
                           moe_w_gate, moe_w_up, moe_w_down, layer, ln_ffn_g[layer], ln_ffn_b[layer])
    return h.reshape(batch, seq_len, d)
```

```python
import functools

import jax
import jax.numpy as jnp
import numpy as np
from jax import lax
from jax.experimental import pallas as pl
from jax.experimental.pallas import tpu as pltpu

F32 = jnp.float32
BF16 = jnp.bfloat16

DEPTH = 2
GLA_HEADS = 4
GLA_DK = 128
GLA_DV = 256
GLA_GATE_RANK = 16
GLA_GATE_TAU = 16.0
GLA_CHUNK = 64
FOX_HEADS = 8
FOX_DH = 128
DSA_Q_HEADS = 16
DSA_KV_HEADS = 4
DSA_DH = 128
IDX_HEADS = 16
IDX_DH = 64
DSA_TOPK_MAX = 256
N_GROUPS = 4
EXPERTS_PER_GROUP = 8
N_EXPERTS = N_GROUPS * EXPERTS_PER_GROUP
ROPE_THETA = 10000.0
LN_EPS = 1e-5
DN_ALPHA = (2 * DEPTH) ** 0.25

LANES = 128
SUBLANES = 8
VMEM_LIMIT_BYTES = 56 * 1024 * 1024

NEG_INF = float("-inf")
LOG2_E = 1.4426950408889634
MASK_BIAS = -1e30


def _params(*semantics):
    return pltpu.CompilerParams(dimension_semantics=semantics, vmem_limit_bytes=VMEM_LIMIT_BYTES)


def _nt_dot(a, b):
    return lax.dot_general(a, b, (((1,), (1,)), ((), ())), preferred_element_type=F32)


def _tn_dot(a, b):
    return lax.dot_general(a, b, (((0,), (0,)), ((), ())), preferred_element_type=F32)


def _log_sigmoid(z):
    return jnp.minimum(z, 0.0) - jnp.log1p(jnp.exp(-jnp.abs(z)))


def _silu(z):
    return z * (1.0 / (1.0 + jnp.exp(-z)))


def _cumsum_rows(x):
    n = x.shape[0]
    row = lax.broadcasted_iota(jnp.int32, x.shape, 0)
    shift = 1
    while shift < n:
        x = x + jnp.where(row >= shift, pltpu.roll(x, shift, 0), 0.0)
        shift *= 2
    return x


def _rot_half(x, head_dim):
    half = head_dim // 2
    n = x.shape[-1]
    lane = lax.broadcasted_iota(jnp.int32, x.shape, 1)
    fwd = pltpu.roll(x, n - half, 1)
    bwd = pltpu.roll(x, half, 1)
    return jnp.where((lane % head_dim) < half, fwd, bwd)


def _layer_norm(y, g, b):
    mu = jnp.mean(y, axis=-1, keepdims=True)
    d = y - mu
    var = jnp.mean(d * d, axis=-1, keepdims=True)
    return d * lax.rsqrt(var + LN_EPS) * g + b


def _cast_kernel(x_ref, o_ref):
    o_ref[...] = x_ref[...].astype(o_ref.dtype)


def _to_bf16(a, rows=512):
    r, c = a.shape
    rows = min(rows, r)
    return pl.pallas_call(
        _cast_kernel,
        grid=(r // rows,),
        in_specs=[pl.BlockSpec((rows, c), lambda i: (i, 0))],
        out_specs=pl.BlockSpec((rows, c), lambda i: (i, 0)),
        out_shape=jax.ShapeDtypeStruct((r, c), BF16),
        compiler_params=_params("parallel"),
        name="to_bf16",
    )(a)


def _proj_kernel(*refs, rope_dim, rope_lanes):
    if rope_dim:
        x_ref, w_ref, cos_ref, sin_ref, o_ref = refs
    else:
        x_ref, w_ref, o_ref = refs
    acc = _nt_dot(x_ref[...], w_ref[...].astype(BF16))
    if rope_dim:
        reps = acc.shape[-1] // LANES
        cos = jnp.tile(cos_ref[...], (1, reps))
        sin = jnp.tile(sin_ref[...], (1, reps))
        roped = acc * cos + _rot_half(acc, rope_dim) * sin
        if rope_lanes < acc.shape[-1]:
            lane = lax.broadcasted_iota(jnp.int32, acc.shape, 1)
            roped = jnp.where(lane < rope_lanes, roped, acc)
        acc = roped
    o_ref[...] = acc.astype(o_ref.dtype)


def _project(x, wt, col_start, n_cols, out_dtype, seq_len, rope=None, rope_lanes=None, tm=1024, tn=512):
    t, k = x.shape
    tm = min(tm, seq_len)
    tn = min(tn, n_cols)
    assert t % tm == 0 and n_cols % tn == 0 and col_start % tn == 0 and seq_len % tm == 0
    col_off = col_start // tn
    in_specs = [
        pl.BlockSpec((tm, k), lambda i, j: (i, 0)),
        pl.BlockSpec((tn, k), lambda i, j: (j + col_off, 0)),
    ]
    args = [x, wt]
    rope_dim = 0
    if rope is not None:
        rope_dim, cos, sin = rope
        seq_tiles = seq_len // tm
        in_specs += [pl.BlockSpec((tm, LANES), lambda i, j: (i % seq_tiles, 0))] * 2
        args += [cos, sin]
    return pl.pallas_call(
        functools.partial(_proj_kernel, rope_dim=rope_dim, rope_lanes=rope_lanes or n_cols),
        grid=(t // tm, n_cols // tn),
        in_specs=in_specs,
        out_specs=pl.BlockSpec((tm, tn), lambda i, j: (i, j)),
        out_shape=jax.ShapeDtypeStruct((t, n_cols), out_dtype),
        compiler_params=_params("parallel", "parallel"),
        name="in_proj",
    )(*args)


def _rope_tables(seq_len, head_dim):
    half = head_dim // 2
    inv = ROPE_THETA ** (-jnp.arange(half, dtype=F32) / half)
    ang = jnp.arange(seq_len, dtype=F32)[:, None] * inv[None, :]
    cos, sin = jnp.cos(ang), jnp.sin(ang)
    reps = LANES // head_dim
    cos_full = jnp.tile(jnp.concatenate([cos, cos], axis=-1), (1, reps))
    sin_signed = jnp.tile(jnp.concatenate([-sin, sin], axis=-1), (1, reps))
    return cos_full, sin_signed


def _out_proj_ln_kernel(oa_ref, ob_ref, w_ref, x_ref, g_ref, b_ref, out_ref, out_bf_ref):
    half = oa_ref.shape[1]
    h = jnp.dot(oa_ref[...], w_ref[0:half, :], preferred_element_type=F32)
    h = h + jnp.dot(ob_ref[...], w_ref[half:2 * half, :], preferred_element_type=F32)
    out = _layer_norm(DN_ALPHA * x_ref[...] + h, g_ref[...], b_ref[...])
    out_ref[...] = out
    out_bf_ref[...] = out.astype(BF16)


def _out_proj_ln(o_a, o_b, b_col_block, w_bf16, x, g, b, tm=512):
    t = o_a.shape[0]
    kdim, d = w_bf16.shape
    half = kdim // 2
    tm = min(tm, t)
    return pl.pallas_call(
        _out_proj_ln_kernel,
        grid=(t // tm,),
        in_specs=[
            pl.BlockSpec((tm, half), lambda i: (i, 0)),
            pl.BlockSpec((tm, half), lambda i: (i, b_col_block)),
            pl.BlockSpec((kdim, d), lambda i: (0, 0)),
            pl.BlockSpec((tm, d), lambda i: (i, 0)),
            pl.BlockSpec((1, d), lambda i: (0, 0)),
            pl.BlockSpec((1, d), lambda i: (0, 0)),
        ],
        out_specs=[pl.BlockSpec((tm, d), lambda i: (i, 0)), pl.BlockSpec((tm, d), lambda i: (i, 0))],
        out_shape=[jax.ShapeDtypeStruct((t, d), F32), jax.ShapeDtypeStruct((t, d), BF16)],
        compiler_params=_params("parallel"),
        name="out_proj_ln",
    )(o_a, o_b, w_bf16, x, g.reshape(1, d), b.reshape(1, d))


def _gla_kernel(q_ref, k_ref, v_ref, r_ref, sm_ref, w2_ref, gb_ref, ng_ref, o_ref, state_ref, *, n_chunks):
    c_len, dk, dv = GLA_CHUNK, GLA_DK, GLA_DV

    @pl.when(pl.program_id(1) == 0)
    def _():
        state_ref[...] = jnp.zeros_like(state_ref)

    row = lax.broadcasted_iota(jnp.int32, (c_len, c_len), 0)
    col = lax.broadcasted_iota(jnp.int32, (c_len, c_len), 1)
    causal = row >= col
    norm_g = ng_ref[...]
    for c in range(n_chunks):
        rows = slice(c * c_len, (c + 1) * c_len)
        glr = sm_ref[rows, FOX_HEADS:FOX_HEADS + GLA_GATE_RANK].astype(BF16)
        for h in range(GLA_HEADS):
            kcols = slice(h * dk, (h + 1) * dk)
            vcols = slice(h * dv, (h + 1) * dv)
            z = jnp.dot(glr, w2_ref[:, kcols].astype(BF16), preferred_element_type=F32) + gb_ref[:, kcols]
            log_a = _log_sigmoid(z) / GLA_GATE_TAU
            b = _cumsum_rows(log_a)
            b_last = b[c_len - 1:c_len, :]
            q = q_ref[rows, kcols] * (dk ** -0.5)
            k = k_ref[rows, kcols]
            v = v_ref[rows, vcols].astype(BF16)
            q_dec = (q * jnp.exp(b)).astype(BF16)
            k_inv = (k * jnp.exp(-b)).astype(BF16)
            k_end = (k * jnp.exp(b_last - b)).astype(BF16)
            scores = jnp.where(causal, _nt_dot(q_dec, k_inv), 0.0)
            state_t = state_ref[h]
            o = jnp.dot(scores.astype(BF16), v, preferred_element_type=F32)
            o = o + _nt_dot(q_dec, state_t.astype(BF16))
            state_ref[h] = jnp.exp(b_last) * state_t + _tn_dot(v, k_end)
            rms = lax.rsqrt(jnp.mean(o * o, axis=-1, keepdims=True) + LN_EPS)
            o_ref[rows, vcols] = (o * rms * norm_g * _silu(r_ref[rows, vcols])).astype(o_ref.dtype)


def _gla(main, small, gate_w2, gate_b, norm_g, batch, seq_len, rows_per_step=256):
    t = main.shape[0]
    n_steps = seq_len // rows_per_step
    hk, hv = GLA_HEADS * GLA_DK, GLA_HEADS * GLA_DV
    row_map = lambda b, n: b * n_steps + n
    return pl.pallas_call(
        functools.partial(_gla_kernel, n_chunks=rows_per_step // GLA_CHUNK),
        grid=(batch, n_steps),
        in_specs=[
            pl.BlockSpec((rows_per_step, hk), lambda b, n: (row_map(b, n), 0)),
            pl.BlockSpec((rows_per_step, hk), lambda b, n: (row_map(b, n), 1)),
            pl.BlockSpec((rows_per_step, hv), lambda b, n: (row_map(b, n), 1)),
            pl.BlockSpec((rows_per_step, hv), lambda b, n: (row_map(b, n), 2)),
            pl.BlockSpec((rows_per_step, LANES), lambda b, n: (row_map(b, n), 0)),
            pl.BlockSpec((GLA_GATE_RANK, hk), lambda b, n: (0, 0)),
            pl.BlockSpec((1, hk), lambda b, n: (0, 0)),
            pl.BlockSpec((1, GLA_DV), lambda b, n: (0, 0)),
        ],
        out_specs=pl.BlockSpec((rows_per_step, hv), lambda b, n: (row_map(b, n), 0)),
        out_shape=jax.ShapeDtypeStruct((t, hv), BF16),
        scratch_shapes=[pltpu.VMEM((GLA_HEADS, GLA_DV, GLA_DK), F32)],
        compiler_params=_params("parallel", "arbitrary"),
        name="gla",
    )(main, main, main, main, small, gate_w2, gate_b.reshape(1, hk), norm_g.reshape(1, GLA_DV))


def _fox_gate_kernel(sm_ref, b_ref, c_ref, ct_ref):
    c = _cumsum_rows(_log_sigmoid(sm_ref[...] + b_ref[...]))
    c_ref[...] = c
    ct_ref[0] = c.T


def _fox_gate(small, gate_b_row, batch, seq_len):
    t = small.shape[0]
    return pl.pallas_call(
        _fox_gate_kernel,
        grid=(batch,),
        in_specs=[pl.BlockSpec((seq_len, LANES), lambda b: (b, 0)), pl.BlockSpec((1, LANES), lambda b: (0, 0))],
        out_specs=[pl.BlockSpec((seq_len, LANES), lambda b: (b, 0)),
                   pl.BlockSpec((1, LANES, seq_len), lambda b: (b, 0, 0))],
        out_shape=[jax.ShapeDtypeStruct((t, LANES), F32), jax.ShapeDtypeStruct((batch, LANES, seq_len), F32)],
        compiler_params=_params("parallel"),
        name="fox_gate",
    )(small, gate_b_row)


def _fox_kernel(q_ref, k_ref, v_ref, cq_ref, ck_ref, o_ref, m_ref, l_ref, acc_ref, *, tq, heads):
    qi = pl.program_id(2)
    dh = FOX_DH
    scale = dh ** -0.5
    inv_scale = dh ** 0.5
    c_exp = scale * LOG2_E
    reps = tq // LANES

    m_ref[...] = jnp.full_like(m_ref, MASK_BIAS)
    l_ref[...] = jnp.zeros_like(l_ref)
    acc_ref[...] = jnp.zeros_like(acc_ref)

    def step(j, masked):
        ks = pl.multiple_of(j * tq, tq)
        for h in range(heads):
            cols = slice(h * dh, (h + 1) * dh)
            q = q_ref[:, cols]
            k = k_ref[pl.ds(ks, tq), cols]
            v = v_ref[pl.ds(ks, tq), cols]
            cq = cq_ref[0, 0, :, h:h + 1] * inv_scale
            s = _nt_dot(q, k) - ck_ref[h, :, pl.ds(ks, tq)] * inv_scale
            if masked:
                row = lax.broadcasted_iota(jnp.int32, s.shape, 0)
                col = lax.broadcasted_iota(jnp.int32, s.shape, 1)
                s = jnp.where(col <= row, s, MASK_BIAS)
            m_prev = m_ref[h]
            m_new = jnp.maximum(m_prev, jnp.max(s, axis=-1, keepdims=True) + cq)
            shift = m_new - cq
            p = jnp.exp2((s - jnp.tile(shift, (1, reps))) * c_exp)
            alpha = jnp.exp2((m_prev - m_new) * c_exp)
            p_sum = p[:, 0:LANES]
            for r in range(1, reps):
                p_sum = p_sum + p[:, r * LANES:(r + 1) * LANES]
            l_ref[h] = alpha * l_ref[h] + p_sum
            acc_ref[h] = alpha * acc_ref[h] + jnp.dot(p.astype(BF16), v, preferred_element_type=F32)
            m_ref[h] = m_new

    def body(j, carry):
        step(j, False)
        return carry

    lax.fori_loop(0, qi, body, 0)
    step(qi, True)
    for h in range(heads):
        l = jnp.sum(l_ref[h], axis=-1, keepdims=True)
        o_ref[:, h * dh:(h + 1) * dh] = (acc_ref[h] / l).astype(o_ref.dtype)


def _fox(fqkv, cq, ck, batch, seq_len, tq=1024, heads=4):
    t = fqkv.shape[0]
    tq = min(tq, seq_len)
    assert seq_len % tq == 0
    nq = seq_len // tq
    n_hg = FOX_HEADS // heads
    w = heads * FOX_DH
    return pl.pallas_call(
        functools.partial(_fox_kernel, tq=tq, heads=heads),
        grid=(batch, n_hg, nq),
        in_specs=[
            pl.BlockSpec((tq, w), lambda b, g, i: (b * nq + i, g)),
            pl.BlockSpec((seq_len, w), lambda b, g, i: (b, n_hg + g)),
            pl.BlockSpec((seq_len, w), lambda b, g, i: (b, 2 * n_hg + g)),
            pl.BlockSpec((1, 1, tq, heads), lambda b, g, i: (b, g, i, 0)),
            pl.BlockSpec((heads, 1, seq_len), lambda b, g, i: (b * n_hg + g, 0, 0)),
        ],
        out_specs=pl.BlockSpec((tq, w), lambda b, g, i: (b * nq + i, g)),
        out_shape=jax.ShapeDtypeStruct((t, FOX_HEADS * FOX_DH), BF16),
        scratch_shapes=[
            pltpu.VMEM((heads, tq, LANES), F32),
            pltpu.VMEM((heads, tq, LANES), F32),
            pltpu.VMEM((heads, tq, FOX_DH), F32),
        ],
        compiler_params=_params("parallel", "parallel", "arbitrary"),
        name="fox_attn",
    )(fqkv, fqkv, fqkv, cq, ck)


def _dsa_kernel(q_ref, k_ref, v_ref, iq_ref, ik_ref, iw_ref, o_ref, sc_ref, m_ref, l_ref, acc_ref,
                *, tq, tk, top_k, max_iters):
    qi = pl.program_id(1)
    n_kc = (qi * tq + tq + tk - 1) // tk
    group = DSA_Q_HEADS // DSA_KV_HEADS
    qpos = qi * tq + lax.broadcasted_iota(jnp.int32, (tq, 1), 0)

    iq = jnp.concatenate([iq_ref[:, h * IDX_DH:(h + 1) * IDX_DH] for h in range(IDX_HEADS)], axis=0)
    iw = iw_ref[:, IDX_DH:IDX_DH + IDX_HEADS] * (IDX_HEADS ** -0.5 * IDX_DH ** -0.5)

    def score_chunk(c, carry):
        lo, hi = carry
        ks = pl.multiple_of(c * tk, tk)
        ik = ik_ref[pl.ds(ks, tk), 0:IDX_DH].astype(BF16)
        rel = _nt_dot(iq, ik)
        score = jnp.zeros((tq, tk), F32)
        for h in range(IDX_HEADS):
            score = score + iw[:, h:h + 1] * jnp.maximum(rel[h * tq:(h + 1) * tq, :], 0.0)
        kpos = ks + lax.broadcasted_iota(jnp.int32, (tq, tk), 1)
        valid = kpos <= qpos
        sc_ref[:, pl.ds(ks, tk)] = jnp.where(valid, score, NEG_INF)
        lo = jnp.minimum(lo, jnp.min(jnp.where(valid, score, jnp.inf), axis=-1, keepdims=True))
        hi = jnp.maximum(hi, jnp.max(jnp.where(valid, score, NEG_INF), axis=-1, keepdims=True))
        return lo, hi

    lo0, hi0 = lax.fori_loop(0, n_kc, score_chunk,
                             (jnp.full((tq, 1), jnp.inf, F32), jnp.full((tq, 1), NEG_INF, F32)))

    lane_reps = tk // LANES

    def count_ge(thr):
        thr_b = jnp.tile(jnp.broadcast_to(thr, (tq, LANES)), (1, lane_reps))

        def body(c, acc):
            ks = pl.multiple_of(c * tk, tk)
            hit = jnp.where(sc_ref[:, pl.ds(ks, tk)] >= thr_b, 1.0, 0.0)
            for r in range(lane_reps):
                acc = acc + hit[:, r * LANES:(r + 1) * LANES]
            return acc
        partial = lax.fori_loop(0, n_kc, body, jnp.zeros((tq, LANES), F32))
        return jnp.sum(partial, axis=-1, keepdims=True)

    k_f = float(top_k)
    n_keys = (qpos + 1).astype(F32)

    def unsettled(cnt_lo):
        return jnp.max(jnp.where(cnt_lo > k_f, 1.0, 0.0)) > 0.0

    def cond(state):
        it, _, _, cnt_lo = state
        return jnp.logical_and(it < max_iters, unsettled(cnt_lo))

    def refine(state):
        it, lo, hi, cnt_lo = state
        mid = 0.5 * (lo + hi)
        cnt = count_ge(mid)
        keep = cnt >= k_f
        lo = jnp.where(keep, mid, lo)
        cnt_lo = jnp.where(keep, cnt, cnt_lo)
        hi = jnp.where(keep, hi, mid)
        return it + 1, lo, hi, cnt_lo

    _, thr, _, _ = lax.while_loop(cond, refine, (jnp.int32(0), lo0, hi0, n_keys))

    m_ref[...] = jnp.full_like(m_ref, MASK_BIAS)
    l_ref[...] = jnp.zeros_like(l_ref)
    acc_ref[...] = jnp.zeros_like(acc_ref)
    c_exp = DSA_DH ** -0.5 * LOG2_E
    thr_b = jnp.tile(jnp.broadcast_to(thr, (tq, LANES)), (1, lane_reps))

    def attend(c, carry):
        ks = pl.multiple_of(c * tk, tk)
        bias = jnp.where(sc_ref[:, pl.ds(ks, tk)] >= thr_b, 0.0, MASK_BIAS)
        for g in range(DSA_KV_HEADS):
            heads = slice(g * group, (g + 1) * group)
            qg = jnp.concatenate(
                [q_ref[:, (g * group + j) * DSA_DH:(g * group + j + 1) * DSA_DH] for j in range(group)], axis=0)
            kc = k_ref[pl.ds(ks, tk), g * DSA_DH:(g + 1) * DSA_DH]
            vc = v_ref[pl.ds(ks, tk), g * DSA_DH:(g + 1) * DSA_DH]
            s = _nt_dot(qg, kc).reshape(group, tq, tk) + bias[None]
            m_prev = m_ref[heads]
            m_new = jnp.maximum(m_prev, jnp.max(s, axis=-1, keepdims=True))
            p = jnp.exp2((s - jnp.tile(m_new, (1, 1, lane_reps))) * c_exp)
            alpha = jnp.exp2((m_prev - m_new) * c_exp)
            p_sum = p[:, :, 0:LANES]
            for r in range(1, lane_reps):
                p_sum = p_sum + p[:, :, r * LANES:(r + 1) * LANES]
            l_ref[heads] = alpha * l_ref[heads] + p_sum
            pv = jnp.dot(p.reshape(group * tq, tk).astype(BF16), vc, preferred_element_type=F32)
            acc_ref[heads] = alpha * acc_ref[heads] + pv.reshape(group, tq, DSA_DH)
            m_ref[heads] = m_new
        return carry

    lax.fori_loop(0, n_kc, attend, 0)
    for h in range(DSA_Q_HEADS):
        l = jnp.sum(l_ref[h], axis=-1, keepdims=True)
        o_ref[:, h * DSA_DH:(h + 1) * DSA_DH] = (acc_ref[h] / l).astype(o_ref.dtype)


def _dsa(qk, v, iq, small, batch, seq_len, tq=256, tk=512):
    t = qk.shape[0]
    tk = min(tk, seq_len)
    nq = seq_len // tq
    qw = DSA_Q_HEADS * DSA_DH
    kw = DSA_KV_HEADS * DSA_DH
    top_k = min(DSA_TOPK_MAX, seq_len // 4)
    return pl.pallas_call(
        functools.partial(_dsa_kernel, tq=tq, tk=tk, top_k=top_k, max_iters=64),
        grid=(batch, nq),
        in_specs=[
            pl.BlockSpec((tq, qw), lambda b, i: (b * nq + i, 0)),
            pl.BlockSpec((seq_len, kw), lambda b, i: (b, qw // kw)),
            pl.BlockSpec((seq_len, kw), lambda b, i: (b, 0)),
            pl.BlockSpec((tq, IDX_HEADS * IDX_DH), lambda b, i: (b * nq + i, 0)),
            pl.BlockSpec((seq_len, LANES), lambda b, i: (b, 0)),
            pl.BlockSpec((tq, LANES), lambda b, i: (b * nq + i, 0)),
        ],
        out_specs=pl.BlockSpec((tq, qw), lambda b, i: (b * nq + i, 0)),
        out_shape=jax.ShapeDtypeStruct((t, qw), BF16),
        scratch_shapes=[
            pltpu.VMEM((tq, seq_len), F32),
            pltpu.VMEM((DSA_Q_HEADS, tq, LANES), F32),
            pltpu.VMEM((DSA_Q_HEADS, tq, LANES), F32),
            pltpu.VMEM((DSA_Q_HEADS, tq, DSA_DH), F32),
        ],
        compiler_params=_params("parallel", "arbitrary"),
        name="dsa",
    )(qk, qk, v, iq, small, small)


def _router_kernel(x_ref, w_ref, b_ref, o_ref, count_ref):
    x = x_ref[...]
    w = w_ref[...]
    x_hi = x.astype(BF16)
    w_hi = w.astype(BF16)
    x_lo = (x - x_hi.astype(F32)).astype(BF16)
    w_lo = (w - w_hi.astype(F32)).astype(BF16)
    logits = (jnp.dot(x_hi, w_hi, preferred_element_type=F32)
              + (jnp.dot(x_hi, w_lo, preferred_element_type=F32) + jnp.dot(x_lo, w_hi, preferred_element_type=F32))
              + b_ref[...])
    lane = lax.broadcasted_iota(jnp.int32, logits.shape, 1)
    big = jnp.int32(LANES)
    g_logits = jnp.where(lane < N_GROUPS, logits, NEG_INF)
    g_max = jnp.max(g_logits, axis=-1, keepdims=True)
    g_prob = 1.0 / jnp.sum(jnp.exp(g_logits - g_max), axis=-1, keepdims=True)
    g_idx = jnp.min(jnp.where(g_logits == g_max, lane, big), axis=-1, keepdims=True)
    first = N_GROUPS + g_idx * EXPERTS_PER_GROUP
    e_logits = jnp.where((lane >= first) & (lane < first + EXPERTS_PER_GROUP), logits, NEG_INF)
    v1 = jnp.max(e_logits, axis=-1, keepdims=True)
    i1 = jnp.min(jnp.where(e_logits == v1, lane, big), axis=-1, keepdims=True)
    rest = jnp.where(lane == i1, NEG_INF, e_logits)
    v2 = jnp.max(rest, axis=-1, keepdims=True)
    i2 = jnp.min(jnp.where(rest == v2, lane, big), axis=-1, keepdims=True)
    e2 = jnp.exp(v2 - v1)
    w1 = (1.0 / (1.0 + e2)) * g_prob
    w2 = (e2 / (1.0 + e2)) * g_prob
    @pl.when(pl.program_id(0) == 0)
    def _():
        count_ref[...] = jnp.zeros_like(count_ref)

    hot1 = jnp.where(lane == i1, 1.0, 0.0)
    hot2 = jnp.where(lane == i2, 1.0, 0.0)
    both = hot1 + hot2
    before = _cumsum_rows(both) - both + count_ref[...]
    rank1 = jnp.sum(before * hot1, axis=-1, keepdims=True)
    rank2 = jnp.sum(before * hot2, axis=-1, keepdims=True)
    count_ref[...] = count_ref[...] + jnp.sum(both, axis=0, keepdims=True)

    out = jnp.where(lane == 0, (i1 - N_GROUPS).astype(F32), 0.0)
    out = jnp.where(lane == 1, (i2 - N_GROUPS).astype(F32), out)
    out = jnp.where(lane == 2, w1, out)
    out = jnp.where(lane == 3, w2, out)
    out = jnp.where(lane == 4, rank1, out)
    out = jnp.where(lane == 5, rank2, out)
    o_ref[...] = out


def _router(x, w_cat, b_cat, tm=512):
    t, d = x.shape
    return pl.pallas_call(
        _router_kernel,
        grid=(t // tm,),
        in_specs=[pl.BlockSpec((tm, d), lambda i: (i, 0)),
                  pl.BlockSpec((d, LANES), lambda i: (0, 0)),
                  pl.BlockSpec((1, LANES), lambda i: (0, 0))],
        out_specs=[pl.BlockSpec((tm, LANES), lambda i: (i, 0)), pl.BlockSpec((1, LANES), lambda i: (0, 0))],
        out_shape=[jax.ShapeDtypeStruct((t, LANES), F32), jax.ShapeDtypeStruct((1, LANES), F32)],
        compiler_params=_params("arbitrary"),
        name="moe_router",
    )(x, w_cat, b_cat)


def _plan_kernel(pos_ref, src_ref, dst_ref, *, n_tokens):
    def body(a, c):
        p = pos_ref[a]
        src_ref[p] = jnp.where(a >= n_tokens, a - n_tokens, a)
        dst_ref[p] = a
        return c

    lax.fori_loop(0, 2 * n_tokens, body, 0, unroll=8)


def _plan(pos, n_tokens):
    smem = pl.BlockSpec(memory_space=pltpu.SMEM)
    return pl.pallas_call(
        functools.partial(_plan_kernel, n_tokens=n_tokens),
        in_specs=[smem],
        out_specs=[smem, smem],
        out_shape=[jax.ShapeDtypeStruct((2 * n_tokens,), jnp.int32)] * 2,
        name="moe_plan",
    )(pos)


def _for_rows(n, tm, fn):
    def body(i, c):
        fn(i)
        return c

    @pl.when(n == tm)
    def _():
        for i in range(tm):
            fn(i)

    @pl.when(n < tm)
    def _():
        lax.fori_loop(0, n, body, 0)


def _experts_kernel(tile_e_ref, tile_start_ref, tile_n_ref, src_ref, dst_ref,
                    x_hbm, wg_ref, wu_ref, wd_ref, y_hbm, xbuf, ybuf, wg_bf, wu_bf, wd_bf, gsem, ssem,
                    *, tm, n_tiles):
    t = pl.program_id(0)
    slot = t % 2

    def gather_row(tile, s, i):
        row = src_ref[tile_start_ref[tile] + i]
        return pltpu.make_async_copy(x_hbm.at[pl.ds(row, 1)], xbuf.at[s, pl.ds(i, 1)], gsem.at[s])

    def scatter_row(tile, s, i):
        row = dst_ref[tile_start_ref[tile] + i]
        return pltpu.make_async_copy(ybuf.at[s, pl.ds(i, 1)], y_hbm.at[pl.ds(row, 1)], ssem.at[s])

    def start_gather(tile, s):
        _for_rows(tile_n_ref[tile], tm, lambda i: gather_row(tile, s, i).start())

    def wait_rows(tile, bulk_copy, row_copy):
        n = tile_n_ref[tile]
        n8 = pl.multiple_of(n // SUBLANES * SUBLANES, SUBLANES)

        @pl.when(n8 > 0)
        def _():
            bulk_copy(n8).wait()

        def tail(i, c):
            row_copy(i).wait()
            return c
        lax.fori_loop(n8, n, tail, 0)

    def wait_gather(tile, s):
        wait_rows(tile,
                  lambda n8: pltpu.make_async_copy(x_hbm.at[pl.ds(0, n8)], xbuf.at[s, pl.ds(0, n8)], gsem.at[s]),
                  lambda i: gather_row(tile, s, i))

    def wait_scatter(tile, s):
        wait_rows(tile,
                  lambda n8: pltpu.make_async_copy(ybuf.at[s, pl.ds(0, n8)], y_hbm.at[pl.ds(0, n8)], ssem.at[s]),
                  lambda i: scatter_row(tile, s, i))

    @pl.when(t == 0)
    def _():
        xbuf[...] = jnp.zeros_like(xbuf)
        start_gather(0, 0)

    @pl.when(t + 1 < n_tiles)
    def _():
        start_gather(t + 1, 1 - slot)

    @pl.when(t >= 2)
    def _():
        wait_scatter(t - 2, slot)

    n_rows = tile_n_ref[t]

    @pl.when(n_rows > 0)
    def _():
        wait_gather(t, slot)

        @pl.when(jnp.logical_or(t == 0, tile_e_ref[t] != tile_e_ref[jnp.maximum(t - 1, 0)]))
        def _():
            wg_bf[...] = wg_ref[0, 0, 0].astype(BF16)
            wu_bf[...] = wu_ref[0, 0, 0].astype(BF16)
            wd_bf[...] = wd_ref[0, 0, 0].astype(BF16)

        xb = xbuf[slot].astype(BF16)
        gate = jnp.dot(xb, wg_bf[...], preferred_element_type=F32)
        up = jnp.dot(xb, wu_bf[...], preferred_element_type=F32)
        hidden = (_silu(gate) * up).astype(BF16)
        ybuf[slot] = jnp.dot(hidden, wd_bf[...], preferred_element_type=F32)
        _for_rows(n_rows, tm, lambda i: scatter_row(t, slot, i).start())

    @pl.when(t == n_tiles - 1)
    def _():
        if n_tiles >= 2:
            wait_scatter(t - 1, 1 - slot)
        wait_scatter(t, slot)


def _experts(x, w_gate, w_up, w_down, layer, tile_e, tile_start, tile_n, src_rows, dst_rows, tm, n_tiles):
    t_tokens, d = x.shape
    ff = w_gate.shape[-1]
    w_map = lambda t, te, ts, tn, sr, ds_: (layer, te[t] // EXPERTS_PER_GROUP, te[t] % EXPERTS_PER_GROUP, 0, 0)
    grid_spec = pltpu.PrefetchScalarGridSpec(
        num_scalar_prefetch=5,
        grid=(n_tiles,),
        in_specs=[
            pl.BlockSpec(memory_space=pl.ANY),
            pl.BlockSpec((1, 1, 1, d, ff), w_map),
            pl.BlockSpec((1, 1, 1, d, ff), w_map),
            pl.BlockSpec((1, 1, 1, ff, d), w_map),
        ],
        out_specs=pl.BlockSpec(memory_space=pl.ANY),
        scratch_shapes=[
            pltpu.VMEM((2, tm, d), F32),
            pltpu.VMEM((2, tm, d), F32),
            pltpu.VMEM((d, ff), BF16),
            pltpu.VMEM((d, ff), BF16),
            pltpu.VMEM((ff, d), BF16),
            pltpu.SemaphoreType.DMA((2,)),
            pltpu.SemaphoreType.DMA((2,)),
        ],
    )
    return pl.pallas_call(
        functools.partial(_experts_kernel, tm=tm, n_tiles=n_tiles),
        grid_spec=grid_spec,
        out_shape=jax.ShapeDtypeStruct((2 * t_tokens, d), F32),
        compiler_params=_params("arbitrary"),
        name="moe_experts",
    )(tile_e, tile_start, tile_n, src_rows, dst_rows, x, w_gate, w_up, w_down)


def _combine_ln_kernel(x_ref, y0_ref, y1_ref, r_ref, g_ref, b_ref, o_ref, o_bf_ref):
    w0 = r_ref[:, 2:3]
    w1 = r_ref[:, 3:4]
    y = DN_ALPHA * x_ref[...] + (w0 * y0_ref[...] + w1 * y1_ref[...])
    out = _layer_norm(y, g_ref[...], b_ref[...])
    o_ref[...] = out
    o_bf_ref[...] = out.astype(BF16)


def _combine_ln(x, y2, route, g, b, tm=512):
    t, d = x.shape
    n_t = t // tm
    return pl.pallas_call(
        _combine_ln_kernel,
        grid=(n_t,),
        in_specs=[
            pl.BlockSpec((tm, d), lambda i: (i, 0)),
            pl.BlockSpec((tm, d), lambda i: (i, 0)),
            pl.BlockSpec((tm, d), lambda i: (i + n_t, 0)),
            pl.BlockSpec((tm, LANES), lambda i: (i, 0)),
            pl.BlockSpec((1, d), lambda i: (0, 0)),
            pl.BlockSpec((1, d), lambda i: (0, 0)),
        ],
        out_specs=[pl.BlockSpec((tm, d), lambda i: (i, 0)), pl.BlockSpec((tm, d), lambda i: (i, 0))],
        out_shape=[jax.ShapeDtypeStruct((t, d), F32), jax.ShapeDtypeStruct((t, d), BF16)],
        compiler_params=_params("parallel"),
        name="moe_combine_ln",
    )(x, y2, y2, route, g.reshape(1, d), b.reshape(1, d))


def _moe_layer(x, wg, bg, we, be, w_gate, w_up, w_down, layer, ln_g, ln_b, tm=256):
    t, d = x.shape
    w_cat = jnp.concatenate([wg, jnp.transpose(we, (1, 0, 2)).reshape(d, N_EXPERTS)], axis=1)
    w_cat = jnp.pad(w_cat, ((0, 0), (0, LANES - w_cat.shape[1])))
    b_cat = jnp.pad(jnp.concatenate([bg, be.reshape(-1)]), (0, LANES - N_GROUPS - N_EXPERTS)).reshape(1, LANES)
    route, lane_counts = _router(x, w_cat, b_cat)

    i32 = jnp.int32
    expert = jnp.concatenate([route[:, 0], route[:, 1]]).astype(i32)
    rank = jnp.concatenate([route[:, 4], route[:, 5]]).astype(i32)
    e_ids = jnp.arange(N_EXPERTS, dtype=i32)
    onehot = (expert[:, None] == e_ids[None, :]).astype(i32)
    counts = lane_counts[0, N_GROUPS:N_GROUPS + N_EXPERTS].astype(i32)
    seg_start = jnp.cumsum(counts) - counts
    pos = jnp.sum(onehot * seg_start[None, :], axis=1) + rank
    tiles_per = (counts + tm - 1) // tm
    tile_end = jnp.cumsum(tiles_per)
    tile_begin = tile_end - tiles_per
    n_tiles = (2 * t) // tm + N_EXPERTS
    tile_ids = jnp.arange(n_tiles, dtype=i32)
    n_used = tile_end[-1]
    tile_e = jnp.sum((tile_end[None, :] <= jnp.minimum(tile_ids, n_used - 1)[:, None]).astype(i32), axis=1)
    tile_e = jnp.clip(tile_e, 0, N_EXPERTS - 1)
    te_onehot = (tile_e[:, None] == e_ids[None, :]).astype(i32)
    within = (tile_ids - jnp.sum(te_onehot * tile_begin[None, :], axis=1)) * tm
    tile_n = jnp.clip(jnp.sum(te_onehot * counts[None, :], axis=1) - within, 0, tm)
    tile_n = jnp.where(tile_ids < n_used, tile_n, 0).astype(i32)
    tile_start = jnp.where(tile_n > 0, jnp.sum(te_onehot * seg_start[None, :], axis=1) + within, 0).astype(i32)
    src_rows, dst_rows = _plan(pos.astype(i32), t)
    y2 = _experts(x, w_gate, w_up, w_down, layer, tile_e.astype(i32), tile_start, tile_n, src_rows, dst_rows,
                  tm, n_tiles)
    return _combine_ln(x, y2, route, ln_g, ln_b)


def _even_mixer_layer(x, xb, batch, seq_len, w_in, gate_w2, gate_b, norm_g, fox_gate_b, w_out, ln_g, ln_b):
    gla_w = 2 * GLA_HEADS * GLA_DK + 2 * GLA_HEADS * GLA_DV
    fox_w = 3 * FOX_HEADS * FOX_DH
    glr0 = gla_w
    fox0 = glr0 + GLA_GATE_RANK
    ff0 = fox0 + fox_w
    wt = jnp.swapaxes(w_in, 0, 1)
    main = _project(xb, wt, 0, gla_w, F32, seq_len)
    fqkv = _project(xb, wt[fox0:ff0], 0, fox_w, BF16, seq_len)
    wt_small = jnp.concatenate([wt[ff0:ff0 + FOX_HEADS], wt[glr0:fox0]], axis=0)
    wt_small = jnp.pad(wt_small, ((0, LANES - wt_small.shape[0]), (0, 0)))
    small = _project(xb, wt_small, 0, LANES, F32, seq_len)
    o_gla = _gla(main, small, gate_w2, gate_b, norm_g, batch, seq_len)
    gate_b_row = jnp.pad(fox_gate_b, (0, LANES - FOX_HEADS)).reshape(1, LANES)
    c, c_t = _fox_gate(small, gate_b_row, batch, seq_len)
    fox_heads_per_step = 4
    cq = c.reshape(batch, seq_len, LANES)[:, :, :FOX_HEADS]
    cq = jnp.transpose(cq.reshape(batch, seq_len, FOX_HEADS // fox_heads_per_step, fox_heads_per_step), (0, 2, 1, 3))
    ck = c_t[:, :FOX_HEADS, :].reshape(batch * FOX_HEADS, 1, seq_len)
    o_fox = _fox(fqkv, cq, ck, batch, seq_len, heads=fox_heads_per_step)
    return _out_proj_ln(o_gla, o_fox, 0, _to_bf16(w_out), x, ln_g, ln_b)


def _odd_mixer_layer(x, xb, batch, seq_len, w_in, w_out, ln_g, ln_b):
    qw = DSA_Q_HEADS * DSA_DH
    kw = DSA_KV_HEADS * DSA_DH
    iqw = IDX_HEADS * IDX_DH
    rope_attn = (DSA_DH,) + _rope_tables(seq_len, DSA_DH)
    rope_idx = (IDX_DH,) + _rope_tables(seq_len, IDX_DH)
    wt = jnp.swapaxes(w_in, 0, 1)
    qk = _project(xb, wt, 0, qw + kw, BF16, seq_len, rope=rope_attn)
    v = _project(xb, wt, qw + kw, kw, BF16, seq_len)
    iq = _project(xb, wt, qw + 2 * kw, iqw, BF16, seq_len, rope=rope_idx)
    small0 = qw + 2 * kw + iqw
    wt_small = jnp.pad(wt[small0:], ((0, LANES - (wt.shape[0] - small0)), (0, 0)))
    small = _project(xb, wt_small, 0, LANES, F32, seq_len, rope=rope_idx, rope_lanes=IDX_DH)
    o = _dsa(qk, v, iq, small, batch, seq_len)
    return _out_proj_ln(o, o, 1, _to_bf16(w_out), x, ln_g, ln_b)


def kernel(x, a_w_in, a_gla_gate_w2, a_gla_gate_b, a_gla_norm_g, a_fox_gate_b, a_w_out, c_w_in, c_w_out,
           ln_mix_g, ln_mix_b, ln_ffn_g, ln_ffn_b, moe_group_w, moe_group_b, moe_expert_w, moe_expert_b,
           moe_w_gate, moe_w_up, moe_w_down):
    batch, seq_len, d = x.shape
    h = x.reshape(batch * seq_len, d)
    hb = _to_bf16(h)
    for layer in range(DEPTH):
        j = layer // 2
        if layer % 2 == 0:
            h, hb = _even_mixer_layer(h, hb, batch, seq_len, a_w_in[j], a_gla_gate_w2[j], a_gla_gate_b[j],
                                      a_gla_norm_g[j], a_fox_gate_b[j], a_w_out[j],
                                      ln_mix_g[layer], ln_mix_b[layer])
        else:
            h, hb = _odd_mixer_layer(h, hb, batch, seq_len, c_w_in[j], c_w_out[j],
                                     ln_mix_g[layer], ln_mix_b[layer])
        h, hb = _moe_layer(h, moe_group_w[layer], moe_group_b[layer], moe_expert_w[layer], moe_expert_b[layer],
                           moe_w_gate, moe_w_up, moe_w_down, layer, ln_ffn_g[layer], ln_ffn_b[layer])
    return h.reshape(batch, seq_len, d)
```
